```python
import jax, jax.numpy as jnp
from jax import lax
import numpy as np

D_MODEL = 1024
BATCH = 16
SEQ = 2048
DEPTH = 4

CHUNK = 64
N_MEM = 256
D_MIX = D_MODEL
N_GROUPS = 4
D_GROUP = D_MIX // N_GROUPS
CONV_WIDTH = 3
POOL_WINDOWS = (2, 4, 8, 16)
POOL_CH = D_GROUP // len(POOL_WINDOWS)
GMLP_BLOCK = 128
GMLP_HEADS = 4
GMLP_HEAD_DIM = D_GROUP // GMLP_HEADS
MEM_HEADS = 4
MEM_HEAD_DIM = D_GROUP // MEM_HEADS
N_EXPERTS = 32
TOP_K = 4
D_FF = D_MODEL
SWIGLU_ALPHA = 1.702
SWIGLU_LIMIT = 7.0
DEEPNORM_ALPHA = (2 * DEPTH) ** 0.25
DEEPNORM_BETA = (8 * DEPTH) ** -0.25
LN_EPS = 1e-5
RMS_EPS = 1e-6
D_IN_PROJ = 7 * D_GROUP
IN_SPLITS = (D_GROUP, 2 * D_GROUP, 3 * D_GROUP, 4 * D_GROUP, 6 * D_GROUP)

kernel_name = "hybrid_conv_pool_gmlp_memattn_moe_deepnorm"


def layer_norm(x, g, b):
    xf = x.astype(jnp.float32)
    mu = jnp.mean(xf, axis=-1, keepdims=True)
    var = jnp.mean(jnp.square(xf - mu), axis=-1, keepdims=True)
    y = (xf - mu) * lax.rsqrt(var + LN_EPS) * g.astype(jnp.float32) + b.astype(jnp.float32)
    return y.astype(x.dtype)


def rms_norm(x, g):
    xf = x.astype(jnp.float32)
    y = xf * lax.rsqrt(jnp.mean(jnp.square(xf), axis=-1, keepdims=True) + RMS_EPS) * g.astype(jnp.float32)
    return y.astype(x.dtype)


def short_conv_mixer(gate_b, gate_c, h, conv_w):
    z = gate_c * h
    y = lax.conv_general_dilated(
        z, conv_w[:, None, :], window_strides=(1,), padding=[(CONV_WIDTH - 1, 0)],
        dimension_numbers=('NWC', 'WIO', 'NWC'), feature_group_count=D_GROUP)
    return gate_b * y


def pool_mixer(p, pool_w, pool_scale):
    _, S, _ = p.shape
    pf = p.astype(jnp.float32)
    csum = jnp.cumsum(pf, axis=1)
    t = jnp.arange(S)
    diffs = []
    for g, w in enumerate(POOL_WINDOWS):
        c = csum[:, :, g * POOL_CH:(g + 1) * POOL_CH]
        prev = jnp.pad(c, ((0, 0), (w, 0), (0, 0)))[:, :S]
        count = jnp.minimum(t + 1, w).astype(jnp.float32)[None, :, None]
        diffs.append((c - prev) / count - pf[:, :, g * POOL_CH:(g + 1) * POOL_CH])
    d = jnp.stack(diffs, axis=2).astype(p.dtype)
    y = jnp.einsum('bsgc,gcd->bsgd', d, pool_w)
    return y.reshape(p.shape) * pool_scale


def spatial_gating_mixer(z, v_ln_g, v_ln_b, w_s, b_s):
    Bsz, S, _ = z.shape
    z = jax.nn.gelu(z)
    u, v = jnp.split(z, 2, axis=-1)
    v = layer_norm(v, v_ln_g, v_ln_b)
    vb = v.reshape(Bsz, S // GMLP_BLOCK, GMLP_BLOCK, GMLP_HEADS, GMLP_HEAD_DIM)
    pos = jnp.arange(GMLP_BLOCK)
    mask = (pos[None, :] // CHUNK) <= (pos[:, None] // CHUNK)
    w_masked = jnp.where(mask[None], w_s, jnp.zeros_like(w_s))
    gate = jnp.einsum('hij,bnjhd->bnihd', w_masked, vb) + b_s.T[:, :, None]
    return u * gate.reshape(Bsz, S, D_GROUP)


def memory_attention(q, k, v):
    Bsz, S, _ = q.shape
    qh = q.reshape(Bsz, S, MEM_HEADS, MEM_HEAD_DIM)
    kh = k.reshape(Bsz, -1, MEM_HEADS, MEM_HEAD_DIM)
    vh = v.reshape(Bsz, -1, MEM_HEADS, MEM_HEAD_DIM)
    s = jnp.einsum('bshd,bmhd->bhsm', qh, kh).astype(jnp.float32) * (MEM_HEAD_DIM ** -0.5)
    p = jax.nn.softmax(s, axis=-1).astype(v.dtype)
    o = jnp.einsum('bhsm,bmhd->bshd', p, vh)
    return o.reshape(Bsz, S, D_GROUP)


def hybrid_mixer(x, mem_n, w_in, conv_w, pool_w, pool_scale, gmlp_ln_g, gmlp_ln_b, gmlp_w, gmlp_b,
                 w_mem_kv, group_norm_g, w_out):
    Bsz, S, _ = x.shape
    proj = jnp.einsum('bsd,dp->bsp', x, w_in)
    gate_b, gate_c, conv_in, pool_in, gmlp_in, mem_q = jnp.split(proj, IN_SPLITS, axis=-1)
    y_conv = short_conv_mixer(gate_b, gate_c, conv_in, conv_w)
    y_pool = pool_mixer(pool_in, pool_w, pool_scale)
    y_gmlp = spatial_gating_mixer(gmlp_in, gmlp_ln_g, gmlp_ln_b, gmlp_w, gmlp_b)
    mem_k, mem_v = jnp.split(jnp.einsum('bmd,dp->bmp', mem_n, w_mem_kv), 2, axis=-1)
    y_mem = memory_attention(mem_q, mem_k, mem_v)
    groups = jnp.stack([y_conv, y_pool, y_gmlp, y_mem], axis=2)
    groups = rms_norm(groups, group_norm_g.reshape(N_GROUPS, D_GROUP))
    return jnp.einsum('bsc,cd->bsd', groups.reshape(Bsz, S, D_MIX), w_out)


def moe_ffn(x, w_router, b_router, w1, b1, w2, b2):
    Bsz, S, D = x.shape
    xt = x.reshape(-1, D)
    logits = (xt @ w_router + b_router).astype(jnp.float32)
    top_val, top_idx = lax.top_k(logits, TOP_K)
    top_w = jax.nn.softmax(top_val, axis=-1)
    gates = jnp.einsum('nk,nke->ne', top_w,
                       jax.nn.one_hot(top_idx, N_EXPERTS, dtype=jnp.float32)).astype(x.dtype)
    y = jnp.zeros_like(xt)
    for e in range(N_EXPERTS):
        h = xt @ w1[e] + b1[e]
        glu = jnp.minimum(h[:, :D_FF], SWIGLU_LIMIT)
        lin = jnp.clip(h[:, D_FF:], -SWIGLU_LIMIT, SWIGLU_LIMIT)
        act = glu * jax.nn.sigmoid(SWIGLU_ALPHA * glu) * (lin + 1)
        y = y + gates[:, e:e + 1] * (act @ w2[e] + b2[e])
    return y.reshape(Bsz, S, D)


def setup_inputs(seed: int = 0) -> dict:
    key = jax.random.key(seed)
    ks = jax.random.split(key, 32)
    f32 = jnp.float32

    def nrm(k, shape, scale):
        return jax.random.normal(k, shape, f32) * scale

    L = DEPTH
    return {
        "x": nrm(ks[0], (BATCH, SEQ, D_MODEL), 1.0),
        "mem": nrm(ks[1], (BATCH, N_MEM, D_MODEL), 1.0),
        "ln_in_g": 1.0 + nrm(ks[2], (D_MODEL,), 0.02),
        "ln_in_b": nrm(ks[3], (D_MODEL,), 0.02),
        "mem_ln_g": 1.0 + nrm(ks[4], (D_MODEL,), 0.02),
        "mem_ln_b": nrm(ks[5], (D_MODEL,), 0.02),
        "w_in": nrm(ks[6], (L, D_MODEL, D_IN_PROJ), D_MODEL ** -0.5),
        "conv_w": nrm(ks[7], (L, CONV_WIDTH, D_GROUP), CONV_WIDTH ** -0.5),
        "pool_w": nrm(ks[8], (L, len(POOL_WINDOWS), POOL_CH, POOL_CH), POOL_CH ** -0.5),
        "pool_scale": 1.0 + nrm(ks[9], (L, D_GROUP), 0.1),
        "gmlp_ln_g": 1.0 + nrm(ks[10], (L, D_GROUP), 0.02),
        "gmlp_ln_b": nrm(ks[11], (L, D_GROUP), 0.02),
        "gmlp_w": nrm(ks[12], (L, GMLP_HEADS, GMLP_BLOCK, GMLP_BLOCK), GMLP_BLOCK ** -0.5),
        "gmlp_b": 1.0 + nrm(ks[13], (L, GMLP_HEADS, GMLP_BLOCK), 0.02),
        "w_mem_kv": nrm(ks[14], (L, D_MODEL, 2 * D_GROUP), D_MODEL ** -0.5),
        "group_norm_g": 1.0 + nrm(ks[15], (L, D_MIX), 0.02),
        "w_out": nrm(ks[16], (L, D_MIX, D_MODEL), D_MIX ** -0.5 * DEEPNORM_BETA),
        "ln1_g": 1.0 + nrm(ks[17], (L, D_MODEL), 0.02),
        "ln1_b": nrm(ks[18], (L, D_MODEL), 0.02),
        "w_router": nrm(ks[19], (L, D_MODEL, N_EXPERTS), D_MODEL ** -0.5),
        "b_router": nrm(ks[20], (L, N_EXPERTS), 0.01),
        "w1": nrm(ks[21], (L, N_EXPERTS, D_MODEL, 2 * D_FF), D_MODEL ** -0.5),
        "b1": nrm(ks[22], (L, N_EXPERTS, 2 * D_FF), 0.02),
        "w2": nrm(ks[23], (L, N_EXPERTS, D_FF, D_MODEL), D_FF ** -0.5 * DEEPNORM_BETA),
        "b2": nrm(ks[24], (L, N_EXPERTS, D_MODEL), 0.02),
        "ln2_g": 1.0 + nrm(ks[25], (L, D_MODEL), 0.02),
        "ln2_b": nrm(ks[26], (L, D_MODEL), 0.02),
    }


def reference(x, mem, ln_in_g, ln_in_b, mem_ln_g, mem_ln_b, w_in, conv_w, pool_w, pool_scale,
              gmlp_ln_g, gmlp_ln_b, gmlp_w, gmlp_b, w_mem_kv, group_norm_g, w_out, ln1_g, ln1_b,
              w_router, b_router, w1, b1, w2, b2, ln2_g, ln2_b):
    h = layer_norm(x, ln_in_g, ln_in_b)
    mem_n = layer_norm(mem, mem_ln_g, mem_ln_b)
    for l in range(DEPTH):
        mix = hybrid_mixer(h, mem_n, w_in[l], conv_w[l], pool_w[l], pool_scale[l], gmlp_ln_g[l],
                           gmlp_ln_b[l], gmlp_w[l], gmlp_b[l], w_mem_kv[l], group_norm_g[l], w_out[l])
        h = layer_norm(DEEPNORM_ALPHA * h + mix, ln1_g[l], ln1_b[l])
        ffn = moe_ffn(h, w_router[l], b_router[l], w1[l], b1[l], w2[l], b2[l])
        h = layer_norm(DEEPNORM_ALPHA * h + ffn, ln2_g[l], ln2_b[l])
    return h
```

```python
import functools

import jax
import jax.numpy as jnp
from jax import lax
from jax.experimental import pallas as pl
from jax.experimental.pallas import tpu as pltpu

F32 = jnp.float32
BF16 = jnp.bfloat16

D_MODEL = 1024
N_GROUPS = 4
D_GROUP = D_MODEL // N_GROUPS
D_IN_PROJ = 7 * D_GROUP
POOL_WINDOWS = (2, 4, 8, 16)
POOL_CH = D_GROUP // len(POOL_WINDOWS)
GMLP_BLOCK = 128
GMLP_HEADS = 4
GMLP_HEAD_DIM = D_GROUP // GMLP_HEADS
CHUNK = 64
MEM_HEADS = 4
MEM_HEAD_DIM = D_GROUP // MEM_HEADS
N_EXPERTS = 32
TOP_K = 4
D_FF = D_MODEL
SWIGLU_ALPHA = 1.702
SWIGLU_LIMIT = 7.0
REF_DEPTH = 4
DEEPNORM_ALPHA = (2 * REF_DEPTH) ** 0.25
LN_EPS = 1e-5
RMS_EPS = 1e-6

SUBLANES = 8
LANES = 128
ROW_TILE_ROWS = D_MODEL // LANES
HALO = max(POOL_WINDOWS)
VMEM_LIMIT_BYTES = 56 * 1024 * 1024

NEG_BIG = -1e30


def _layer_norm(x, g, b):
    mu = jnp.mean(x, axis=-1, keepdims=True)
    xc = x - mu
    var = jnp.mean(xc * xc, axis=-1, keepdims=True)
    return xc * lax.rsqrt(var + LN_EPS) * g + b


def _ln_kernel(x_ref, g_ref, b_ref, o_ref):
    o_ref[...] = _layer_norm(x_ref[...], g_ref[...], b_ref[...])


def _ln_rows(x, g, b, rows):
    n, d = x.shape
    return pl.pallas_call(
        _ln_kernel,
        grid=(n // rows,),
        in_specs=[pl.BlockSpec((rows, d), lambda i: (i, 0)),
                  pl.BlockSpec((1, d), lambda i: (0, 0)),
                  pl.BlockSpec((1, d), lambda i: (0, 0))],
        out_specs=pl.BlockSpec((rows, d), lambda i: (i, 0)),
        out_shape=jax.ShapeDtypeStruct((n, d), F32),
        name="entry_ln",
    )(x, g.reshape(1, d), b.reshape(1, d))


def _mem_kv_kernel(m_ref, g_ref, b_ref, w_ref, o_ref):
    mn = _layer_norm(m_ref[0], g_ref[...], b_ref[...])
    o_ref[0] = jnp.dot(mn.astype(BF16), w_ref[...], preferred_element_type=F32).astype(BF16)


def _mem_kv(mem, g, b, w_all):
    bsz, m, d = mem.shape
    p = w_all.shape[1]
    return pl.pallas_call(
        _mem_kv_kernel,
        grid=(bsz,),
        in_specs=[pl.BlockSpec((1, m, d), lambda i: (i, 0, 0)),
                  pl.BlockSpec((1, d), lambda i: (0, 0)),
                  pl.BlockSpec((1, d), lambda i: (0, 0)),
                  pl.BlockSpec((d, p), lambda i: (0, 0))],
        out_specs=pl.BlockSpec((1, m, p), lambda i: (i, 0, 0)),
        out_shape=jax.ShapeDtypeStruct((bsz, m, p), BF16),
        name="mem_kv",
    )(mem, g.reshape(1, d), b.reshape(1, d), w_all)


def _mixer_kernel(x_ref, kv_ref, win_ref, convw_ref, poolw_ref, pscale_ref, glng_ref, glnb_ref,
                  gw_ref, gbias_ref, gng_ref, wout_ref, ln1g_ref, ln1b_ref, wrh_ref, wrl_ref, br_ref,
                  h1rt_ref, idx_ref, gate_ref, ext_ref, *, ts):
    s = pl.program_id(1)
    dg = D_GROUP

    x = x_ref[...]
    proj = jnp.dot(x.astype(BF16), win_ref[...], preferred_element_type=F32)
    gate_b = proj[:, 0:dg]
    gate_c = proj[:, dg:2 * dg]
    conv_in = proj[:, 2 * dg:3 * dg]
    pool_in = proj[:, 3 * dg:4 * dg]
    gmlp_in = proj[:, 4 * dg:6 * dg]
    mem_q = proj[:, 6 * dg:7 * dg]

    @pl.when(s == 0)
    def _():
        ext_ref[0:HALO, :] = jnp.zeros((HALO, 2 * dg), F32)

    z = gate_c * conv_in
    ext_ref[HALO:HALO + ts, 0:dg] = z
    ext_ref[HALO:HALO + ts, dg:2 * dg] = pool_in

    zm1 = ext_ref[pl.ds(HALO - 1, ts), 0:dg]
    zm2 = ext_ref[pl.ds(HALO - 2, ts), 0:dg]
    cw = convw_ref[...]
    y_conv = gate_b * (cw[0:1, :] * zm2 + cw[1:2, :] * zm1 + cw[2:3, :] * z)

    pos = s * ts + lax.broadcasted_iota(jnp.int32, (ts, 1), 0)
    half = dg // 2
    lane_h = lax.broadcasted_iota(jnp.int32, (ts, half), 1)
    diffs = []
    for hf in range(2):
        c0 = dg + hf * half
        w_lo, w_hi = POOL_WINDOWS[2 * hf], POOL_WINDOWS[2 * hf + 1]
        cur = ext_ref[pl.ds(HALO, ts), c0:c0 + half]
        acc = cur
        for k in range(1, w_lo):
            acc = acc + ext_ref[pl.ds(HALO - k, ts), c0:c0 + half]
        s_lo = acc
        for k in range(w_lo, w_hi):
            acc = acc + ext_ref[pl.ds(HALO - k, ts), c0:c0 + half]
        s_hi = acc
        cnt_lo = jnp.minimum(pos + 1, w_lo).astype(F32)
        cnt_hi = jnp.minimum(pos + 1, w_hi).astype(F32)
        mean = jnp.where(lane_h < POOL_CH, s_lo / cnt_lo, s_hi / cnt_hi)
        diffs.append(mean - cur)
    dpool = jnp.concatenate(diffs, axis=1)
    y_pool = jnp.dot(dpool.astype(BF16), poolw_ref[...], preferred_element_type=F32) * pscale_ref[...]

    ext_ref[0:HALO, :] = ext_ref[ts:ts + HALO, :]

    zg = jax.nn.gelu(gmlp_in, approximate=True)
    u = zg[:, 0:dg]
    v = _layer_norm(zg[:, dg:2 * dg], glng_ref[...], glnb_ref[...]).astype(BF16)
    lane_g = lax.broadcasted_iota(jnp.int32, (GMLP_BLOCK, dg), 1)
    gw = gw_ref[...]
    gbias = gbias_ref[...]
    gates = []
    for nb in range(ts // GMLP_BLOCK):
        vb = v[nb * GMLP_BLOCK:(nb + 1) * GMLP_BLOCK, :]
        r = jnp.dot(gw, vb, preferred_element_type=F32)
        g = r[0:GMLP_BLOCK, :]
        for h in range(1, GMLP_HEADS):
            g = jnp.where(lane_g >= h * GMLP_HEAD_DIM, r[h * GMLP_BLOCK:(h + 1) * GMLP_BLOCK, :], g)
        gates.append(g + gbias)
    y_gmlp = u * jnp.concatenate(gates, axis=0)

    kv = kv_ref[0]
    mk = kv[:, 0:dg]
    mv = kv[:, dg:2 * dg]
    lane_q = lax.broadcasted_iota(jnp.int32, (ts, dg), 1)
    qs = mem_q * (MEM_HEAD_DIM ** -0.5)
    y_mem = jnp.zeros((ts, dg), F32)
    for h in range(MEM_HEADS):
        in_head = (lane_q >= h * MEM_HEAD_DIM) & (lane_q < (h + 1) * MEM_HEAD_DIM)
        qh = jnp.where(in_head, qs, 0.0).astype(BF16)
        sc = lax.dot_general(qh, mk, (((1,), (1,)), ((), ())), preferred_element_type=F32)
        e = jnp.exp(sc - jnp.max(sc, axis=-1, keepdims=True))
        den = jnp.sum(e, axis=-1, keepdims=True)
        oh = jnp.dot(e.astype(BF16), mv, preferred_element_type=F32) / den
        y_mem = jnp.where(in_head, oh, y_mem)

    gng = gng_ref[...]
    groups = []
    for gi, y in enumerate((y_conv, y_pool, y_gmlp, y_mem)):
        ms = jnp.mean(y * y, axis=-1, keepdims=True)
        groups.append(y * lax.rsqrt(ms + RMS_EPS) * gng[:, gi * dg:(gi + 1) * dg])
    cat = jnp.concatenate(groups, axis=1).astype(BF16)
    mix = jnp.dot(cat, wout_ref[...], preferred_element_type=F32)

    h1 = _layer_norm(DEEPNORM_ALPHA * x + mix, ln1g_ref[...], ln1b_ref[...])
    for j in range(ROW_TILE_ROWS):
        h1rt_ref[pl.ds(j, ts, stride=ROW_TILE_ROWS), :] = h1[:, j * LANES:(j + 1) * LANES]

    h1_hi = h1.astype(BF16)
    h1_lo = (h1 - h1_hi.astype(F32)).astype(BF16)
    logits = (jnp.dot(h1_hi, wrh_ref[...], preferred_element_type=F32)
              + jnp.dot(h1_lo, wrh_ref[...], preferred_element_type=F32)
              + jnp.dot(h1_hi, wrl_ref[...], preferred_element_type=F32)) + br_ref[...]
    lane = lax.broadcasted_iota(jnp.int32, (ts, LANES), 1)
    lane_f = lane.astype(F32)
    vals, idxs = [], []
    cur_l = logits
    for _ in range(TOP_K):
        m = jnp.max(cur_l, axis=-1, keepdims=True)
        i_f = jnp.min(jnp.where(cur_l == m, lane_f, float(LANES)), axis=-1, keepdims=True)
        vals.append(m)
        idxs.append(i_f)
        cur_l = jnp.where(lane_f == i_f, -jnp.inf, cur_l)
    es = [jnp.exp(vk - vals[0]) for vk in vals]
    den = es[0] + es[1] + es[2] + es[3]
    idx_out = jnp.zeros((ts, LANES), F32)
    gate_out = jnp.zeros((ts, LANES), F32)
    for k in range(TOP_K):
        idx_out = jnp.where(lane == k, idxs[k], idx_out)
        gate_out = jnp.where(lane == k, es[k] / den, gate_out)
    idx_ref[...] = idx_out.astype(jnp.int32)
    gate_ref[...] = gate_out


def _mixer(h, kv_all, layer, lw, *, bsz, seq, ts):
    n = bsz * seq
    st = seq // ts
    dg = D_GROUP
    const = lambda b, s: (0, 0)
    kern = functools.partial(_mixer_kernel, ts=ts)
    return pl.pallas_call(
        kern,
        grid=(bsz, st),
        in_specs=[
            pl.BlockSpec((ts, D_MODEL), lambda b, s: (b * st + s, 0)),
            pl.BlockSpec((1, kv_all.shape[1], 2 * dg), lambda b, s: (b, 0, layer)),
            pl.BlockSpec((D_MODEL, D_IN_PROJ), const),
            pl.BlockSpec((3, dg), const),
            pl.BlockSpec((dg, dg), const),
            pl.BlockSpec((1, dg), const),
            pl.BlockSpec((1, dg), const),
            pl.BlockSpec((1, dg), const),
            pl.BlockSpec((GMLP_HEADS * GMLP_BLOCK, GMLP_BLOCK), const),
            pl.BlockSpec((GMLP_BLOCK, dg), const),
            pl.BlockSpec((1, D_MODEL), const),
            pl.BlockSpec((D_MODEL, D_MODEL), const),
            pl.BlockSpec((1, D_MODEL), const),
            pl.BlockSpec((1, D_MODEL), const),
            pl.BlockSpec((D_MODEL, LANES), const),
            pl.BlockSpec((D_MODEL, LANES), const),
            pl.BlockSpec((1, LANES), const),
        ],
        out_specs=[
            pl.BlockSpec((ts * ROW_TILE_ROWS, LANES), lambda b, s: (b * st + s, 0)),
            pl.BlockSpec((ts, LANES), lambda b, s: (b * st + s, 0)),
            pl.BlockSpec((ts, LANES), lambda b, s: (b * st + s, 0)),
        ],
        out_shape=[
            jax.ShapeDtypeStruct((n * ROW_TILE_ROWS, LANES), F32),
            jax.ShapeDtypeStruct((n, LANES), jnp.int32),
            jax.ShapeDtypeStruct((n, LANES), F32),
        ],
        scratch_shapes=[pltpu.VMEM((ts + HALO, 2 * dg), F32)],
        compiler_params=pltpu.CompilerParams(
            dimension_semantics=("arbitrary", "arbitrary"), vmem_limit_bytes=VMEM_LIMIT_BYTES),
        name="mixer",
    )(h, kv_all, *lw)


def _expert_kernel(te_ref, src_cur_ref, src_nxt_ref, dst_ref, h_hbm, w1_ref, b1_ref, w2_ref, b2_ref, ys_hbm,
                   xbuf, obuf, w1b, w2b, gsem, ssem, *, tm):
    i = pl.program_id(0)
    nt = pl.num_programs(0)
    slot = lax.rem(i, 2)
    rows = tm * ROW_TILE_ROWS

    def gather_row(src_ref, r, sl):
        t = src_ref[0, 0, r]
        return pltpu.make_async_copy(
            h_hbm.at[pl.ds(pl.multiple_of(t * ROW_TILE_ROWS, ROW_TILE_ROWS), ROW_TILE_ROWS), :],
            xbuf.at[sl, pl.ds(r * ROW_TILE_ROWS, ROW_TILE_ROWS), :], gsem.at[sl])

    def scatter_row(r, sl):
        t = dst_ref[0, 0, r]
        return pltpu.make_async_copy(
            obuf.at[sl, pl.ds(r * ROW_TILE_ROWS, ROW_TILE_ROWS), :],
            ys_hbm.at[pl.ds(pl.multiple_of(t * ROW_TILE_ROWS, ROW_TILE_ROWS), ROW_TILE_ROWS), :], ssem.at[sl])

    def wait_gather(sl):
        pltpu.make_async_copy(h_hbm.at[pl.ds(0, rows), :], xbuf.at[sl], gsem.at[sl]).wait()

    def wait_scatter(sl):
        pltpu.make_async_copy(obuf.at[sl], ys_hbm.at[pl.ds(0, rows), :], ssem.at[sl]).wait()

    @pl.when(i == 0)
    def _():
        for r in range(tm):
            gather_row(src_cur_ref, r, 0).start()

    for r in range(tm):
        gather_row(src_nxt_ref, r, 1 - slot).start()

    changed = jnp.logical_or(i == 0, te_ref[i] != te_ref[jnp.maximum(i - 1, 0)])

    @pl.when(changed)
    def _():
        w1b[...] = w1_ref[...].astype(BF16)
        w2b[...] = w2_ref[...].astype(BF16)

    wait_gather(slot)
    xs = xbuf.at[slot]
    x = jnp.concatenate([xs[pl.ds(j, tm, stride=ROW_TILE_ROWS), :] for j in range(ROW_TILE_ROWS)], axis=1)
    h = jnp.dot(x.astype(BF16), w1b[...], preferred_element_type=F32) + b1_ref[...]
    glu = jnp.minimum(h[:, 0:D_FF], SWIGLU_LIMIT)
    lin = jnp.clip(h[:, D_FF:2 * D_FF], -SWIGLU_LIMIT, SWIGLU_LIMIT)
    act = glu * jax.nn.sigmoid(SWIGLU_ALPHA * glu) * (lin + 1.0)
    out = jnp.dot(act.astype(BF16), w2b[...], preferred_element_type=F32) + b2_ref[...]

    @pl.when(i >= 2)
    def _():
        wait_scatter(slot)

    os_ = obuf.at[slot]
    for j in range(ROW_TILE_ROWS):
        os_[pl.ds(j, tm, stride=ROW_TILE_ROWS), :] = out[:, j * LANES:(j + 1) * LANES]
    for r in range(tm):
        scatter_row(r, slot).start()

    @pl.when(i == nt - 1)
    def _():
        wait_gather(1 - slot)

        @pl.when(i >= 1)
        def _():
            wait_scatter(1 - slot)

        wait_scatter(slot)


def _experts(h1rt, te, src, dst, w1, b1, w2, b2, layer, *, tm, n_slots):
    nt = src.shape[0]
    rows = tm * ROW_TILE_ROWS
    kern = functools.partial(_expert_kernel, tm=tm)
    smem_blk = lambda f: pl.BlockSpec((1, 1, tm), f, memory_space=pltpu.SMEM)
    grid_spec = pltpu.PrefetchScalarGridSpec(
        num_scalar_prefetch=1,
        grid=(nt,),
        in_specs=[
            smem_blk(lambda i, te: (i, 0, 0)),
            smem_blk(lambda i, te: (jnp.minimum(i + 1, nt - 1), 0, 0)),
            smem_blk(lambda i, te: (i, 0, 0)),
            pl.BlockSpec(memory_space=pl.ANY),
            pl.BlockSpec((None, None, D_MODEL, 2 * D_FF), lambda i, te: (layer, te[i], 0, 0)),
            pl.BlockSpec((None, None, 1, 2 * D_FF), lambda i, te: (layer, te[i], 0, 0)),
            pl.BlockSpec((None, None, D_FF, D_MODEL), lambda i, te: (layer, te[i], 0, 0)),
            pl.BlockSpec((None, None, 1, D_MODEL), lambda i, te: (layer, te[i], 0, 0)),
        ],
        out_specs=pl.BlockSpec(memory_space=pl.ANY),
        scratch_shapes=[
            pltpu.VMEM((2, rows, LANES), F32),
            pltpu.VMEM((2, rows, LANES), F32),
            pltpu.VMEM((D_MODEL, 2 * D_FF), BF16),
            pltpu.VMEM((D_FF, D_MODEL), BF16),
            pltpu.SemaphoreType.DMA((2,)),
            pltpu.SemaphoreType.DMA((2,)),
        ],
    )
    return pl.pallas_call(
        kern,
        grid_spec=grid_spec,
        out_shape=jax.ShapeDtypeStruct((n_slots * ROW_TILE_ROWS, LANES), F32),
        compiler_params=pltpu.CompilerParams(
            dimension_semantics=("arbitrary",), vmem_limit_bytes=VMEM_LIMIT_BYTES),
        name="experts",
    )(te, src, src, dst, h1rt, w1, b1, w2, b2)


def _combine_kernel(y0_ref, y1_ref, y2_ref, y3_ref, h1rt_ref, gate_ref, g_ref, b_ref, o_ref, *, ts):
    gates = gate_ref[...]
    cols = []
    for j in range(ROW_TILE_ROWS):
        acc = DEEPNORM_ALPHA * h1rt_ref[pl.ds(j, ts, stride=ROW_TILE_ROWS), :]
        for k, y_ref in enumerate((y0_ref, y1_ref, y2_ref, y3_ref)):
            acc = acc + gates[:, k:k + 1] * y_ref[pl.ds(j, ts, stride=ROW_TILE_ROWS), :]
        cols.append(acc)
    r = jnp.concatenate(cols, axis=1)
    o_ref[...] = _layer_norm(r, g_ref[...], b_ref[...])


def _combine(ys, h1rt, gates, g, b, *, n, ts):
    nb = n // ts
    rows = ts * ROW_TILE_ROWS
    kern = functools.partial(_combine_kernel, ts=ts)
    y_spec = lambda k: pl.BlockSpec((rows, LANES), lambda i: (k * nb + i, 0))
    return pl.pallas_call(
        kern,
        grid=(nb,),
        in_specs=[y_spec(0), y_spec(1), y_spec(2), y_spec(3),
                  pl.BlockSpec((rows, LANES), lambda i: (i, 0)),
                  pl.BlockSpec((ts, LANES), lambda i: (i, 0)),
                  pl.BlockSpec((1, D_MODEL), lambda i: (0, 0)),
                  pl.BlockSpec((1, D_MODEL), lambda i: (0, 0))],
        out_specs=pl.BlockSpec((ts, D_MODEL), lambda i: (i, 0)),
        out_shape=jax.ShapeDtypeStruct((n, D_MODEL), F32),
        compiler_params=pltpu.CompilerParams(
            dimension_semantics=("arbitrary",), vmem_limit_bytes=VMEM_LIMIT_BYTES),
        name="combine",
    )(ys, ys, ys, ys, h1rt, gates, g.reshape(1, D_MODEL), b.reshape(1, D_MODEL))


def _routing_plan(top_idx, n, tm):
    na = n * TOP_K
    n_tiles = na // tm + N_EXPERTS
    p_rows = n_tiles * tm
    e_flat = top_idx.reshape(na)
    order = jnp.argsort(e_flat, stable=True).astype(jnp.int32)
    counts = jnp.sum((e_flat[:, None] == jnp.arange(N_EXPERTS, dtype=jnp.int32)[None, :]).astype(jnp.int32), axis=0)
    uend = jnp.cumsum(counts)
    ustart = uend - counts
    gsz = ((counts + tm - 1) // tm) * tm
    pend = jnp.cumsum(gsz)
    pstart = pend - gsz
    p = jnp.arange(p_rows, dtype=jnp.int32)
    e_p = jnp.sum((p[:, None] >= pend[None, :]).astype(jnp.int32), axis=1)
    e_c = jnp.minimum(e_p, N_EXPERTS - 1)
    r = p - pstart[e_c]
    valid = (e_p < N_EXPERTS) & (r < counts[e_c])
    a = order[jnp.clip(ustart[e_c] + r, 0, na - 1)]
    tok = a // TOP_K
    k = a - tok * TOP_K
    uend_ext = jnp.concatenate([uend, jnp.full((1,), na, jnp.int32)])
    src = jnp.where(valid, tok, 0)
    dst = jnp.where(valid, k * n + tok, na + p - uend_ext[e_p])
    te = e_c[::tm]
    return te, src.reshape(n_tiles, 1, tm), dst.reshape(n_tiles, 1, tm), p_rows


def _pick_tile(total, want):
    t = min(total, want)
    while total % t:
        t //= 2
    return t


def kernel(x, mem, ln_in_g, ln_in_b, mem_ln_g, mem_ln_b, w_in, conv_w, pool_w, pool_scale, gmlp_ln_g, gmlp_ln_b,
           gmlp_w, gmlp_b, w_mem_kv, group_norm_g, w_out, ln1_g, ln1_b, w_router, b_router, w1, b1, w2, b2,
           ln2_g, ln2_b):
    bsz, seq, d = x.shape
    depth = w_in.shape[0]
    n = bsz * seq
    dg = D_GROUP
    ts = _pick_tile(seq, 512)
    tm = 256

    h = _ln_rows(x.reshape(n, d), ln_in_g, ln_in_b, _pick_tile(n, 1024))

    wkv_all = jnp.transpose(w_mem_kv, (1, 0, 2)).reshape(d, depth * 2 * dg).astype(BF16)
    kv_all = _mem_kv(mem, mem_ln_g, mem_ln_b, wkv_all)

    pos = jnp.arange(GMLP_BLOCK)
    mask = (pos[None, :] // CHUNK) <= (pos[:, None] // CHUNK)
    eye = jnp.eye(len(POOL_WINDOWS), dtype=F32)

    wr_pad = jnp.zeros((depth, d, LANES), F32).at[:, :, :N_EXPERTS].set(w_router)
    wr_hi = wr_pad.astype(BF16)
    wr_lo = (wr_pad - wr_hi.astype(F32)).astype(BF16)
    br_pad = jnp.full((depth, 1, LANES), NEG_BIG, F32).at[:, 0, :N_EXPERTS].set(b_router)
    b1r = b1.reshape(depth, N_EXPERTS, 1, 2 * D_FF)
    b2r = b2.reshape(depth, N_EXPERTS, 1, d)

    for l in range(depth):
        pool_bd = jnp.einsum('gh,gcd->gchd', eye, pool_w[l]).reshape(dg, dg).astype(BF16)
        gw = jnp.where(mask[None], gmlp_w[l], 0.0).reshape(GMLP_HEADS * GMLP_BLOCK, GMLP_BLOCK).astype(BF16)
        gbias = jnp.repeat(gmlp_b[l].T, GMLP_HEAD_DIM, axis=1)
        lw = (w_in[l].astype(BF16), conv_w[l], pool_bd, pool_scale[l].reshape(1, dg),
              gmlp_ln_g[l].reshape(1, dg), gmlp_ln_b[l].reshape(1, dg), gw, gbias,
              group_norm_g[l].reshape(1, d), w_out[l].astype(BF16),
              ln1_g[l].reshape(1, d), ln1_b[l].reshape(1, d), wr_hi[l], wr_lo[l], br_pad[l])
        h1rt, top_idx, gates = _mixer(h, kv_all, l, lw, bsz=bsz, seq=seq, ts=ts)
        te, src, dst, p_rows = _routing_plan(top_idx[:, :TOP_K], n, tm)
        ys = _experts(h1rt, te, src, dst, w1, b1r, w2, b2r, l, tm=tm, n_slots=p_rows)
        h = _combine(ys, h1rt, gates, ln2_g[l], ln2_b[l], n=n, ts=ts)
    return h.reshape(bsz, seq, d)
```

```python
import functools

import jax
import jax.numpy as jnp
from jax import lax
from jax.experimental import pallas as pl
from jax.experimental.pallas import tpu as pltpu

F32 = jnp.float32
BF16 = jnp.bfloat16

D_MODEL = 1024
N_GROUPS = 4
D_GROUP = D_MODEL // N_GROUPS
D_IN_PROJ = 7 * D_GROUP
POOL_WINDOWS = (2, 4, 8, 16)
POOL_CH = D_GROUP // len(POOL_WINDOWS)
GMLP_BLOCK = 128
GMLP_HEADS = 4
GMLP_HEAD_DIM = D_GROUP // GMLP_HEADS
CHUNK = 64
MEM_HEADS = 4
MEM_HEAD_DIM = D_GROUP // MEM_HEADS
N_EXPERTS = 32
TOP_K = 4
D_FF = D_MODEL
SWIGLU_ALPHA = 1.702
SWIGLU_LIMIT = 7.0
REF_DEPTH = 4
DEEPNORM_ALPHA = (2 * REF_DEPTH) ** 0.25
LN_EPS = 1e-5
RMS_EPS = 1e-6

SUBLANES = 8
LANES = 128
ROW_TILE_ROWS = D_MODEL // LANES
HALO = max(POOL_WINDOWS)
VMEM_LIMIT_BYTES = 56 * 1024 * 1024
EXPERT_CHUNKS = 4

NEG_BIG = -1e30


def _layer_norm(x, g, b):
    mu = jnp.mean(x, axis=-1, keepdims=True)
    xc = x - mu
    var = jnp.mean(xc * xc, axis=-1, keepdims=True)
    return xc * lax.rsqrt(var + LN_EPS) * g + b


def _ln_kernel(x_ref, g_ref, b_ref, o_ref):
    o_ref[...] = _layer_norm(x_ref[...], g_ref[...], b_ref[...])


def _ln_rows(x, g, b, rows):
    n, d = x.shape
    return pl.pallas_call(
        _ln_kernel,
        grid=(n // rows,),
        in_specs=[pl.BlockSpec((rows, d), lambda i: (i, 0)),
                  pl.BlockSpec((1, d), lambda i: (0, 0)),
                  pl.BlockSpec((1, d), lambda i: (0, 0))],
        out_specs=pl.BlockSpec((rows, d), lambda i: (i, 0)),
        out_shape=jax.ShapeDtypeStruct((n, d), F32),
        name="entry_ln",
    )(x, g.reshape(1, d), b.reshape(1, d))


def _mem_kv_kernel(m_ref, g_ref, b_ref, w_ref, o_ref):
    mn = _layer_norm(m_ref[0], g_ref[...], b_ref[...])
    o_ref[0] = jnp.dot(mn.astype(BF16), w_ref[...], preferred_element_type=F32).astype(BF16)


def _mem_kv(mem, g, b, w_all):
    bsz, m, d = mem.shape
    p = w_all.shape[1]
    return pl.pallas_call(
        _mem_kv_kernel,
        grid=(bsz,),
        in_specs=[pl.BlockSpec((1, m, d), lambda i: (i, 0, 0)),
                  pl.BlockSpec((1, d), lambda i: (0, 0)),
                  pl.BlockSpec((1, d), lambda i: (0, 0)),
                  pl.BlockSpec((d, p), lambda i: (0, 0))],
        out_specs=pl.BlockSpec((1, m, p), lambda i: (i, 0, 0)),
        out_shape=jax.ShapeDtypeStruct((bsz, m, p), BF16),
        name="mem_kv",
    )(mem, g.reshape(1, d), b.reshape(1, d), w_all)


def _mixer_kernel(x_ref, kv_ref, win_ref, convw_ref, poolw_ref, pscale_ref, glng_ref, glnb_ref,
                  gw_ref, gbias_ref, gng_ref, wout_ref, ln1g_ref, ln1b_ref, wrh_ref, wrl_ref, br_ref,
                  h1rt_ref, key_ref, gate_ref, cnt_ref, ext_ref, *, ts, key_shift):
    s = pl.program_id(1)
    dg = D_GROUP

    x = x_ref[...]
    proj = jnp.dot(x.astype(BF16), win_ref[...], preferred_element_type=F32)
    gate_b = proj[:, 0:dg]
    gate_c = proj[:, dg:2 * dg]
    conv_in = proj[:, 2 * dg:3 * dg]
    pool_in = proj[:, 3 * dg:4 * dg]
    gmlp_in = proj[:, 4 * dg:6 * dg]
    mem_q = proj[:, 6 * dg:7 * dg]

    @pl.when(s == 0)
    def _():
        ext_ref[0:HALO, :] = jnp.zeros((HALO, 2 * dg), F32)

    z = gate_c * conv_in
    ext_ref[HALO:HALO + ts, 0:dg] = z
    ext_ref[HALO:HALO + ts, dg:2 * dg] = pool_in

    zm1 = ext_ref[pl.ds(HALO - 1, ts), 0:dg]
    zm2 = ext_ref[pl.ds(HALO - 2, ts), 0:dg]
    cw = convw_ref[...]
    y_conv = gate_b * (cw[0:1, :] * zm2 + cw[1:2, :] * zm1 + cw[2:3, :] * z)

    pos = s * ts + lax.broadcasted_iota(jnp.int32, (ts, 1), 0)
    half = dg // 2
    lane_h = lax.broadcasted_iota(jnp.int32, (ts, half), 1)
    diffs = []
    for hf in range(2):
        c0 = dg + hf * half
        w_lo, w_hi = POOL_WINDOWS[2 * hf], POOL_WINDOWS[2 * hf + 1]
        cur = ext_ref[pl.ds(HALO, ts), c0:c0 + half]
        acc = cur
        for k in range(1, w_lo):
            acc = acc + ext_ref[pl.ds(HALO - k, ts), c0:c0 + half]
        s_lo = acc
        for k in range(w_lo, w_hi):
            acc = acc + ext_ref[pl.ds(HALO - k, ts), c0:c0 + half]
        s_hi = acc
        cnt_lo = jnp.minimum(pos + 1, w_lo).astype(F32)
        cnt_hi = jnp.minimum(pos + 1, w_hi).astype(F32)
        mean = jnp.where(lane_h < POOL_CH, s_lo / cnt_lo, s_hi / cnt_hi)
        diffs.append(mean - cur)
    dpool = jnp.concatenate(diffs, axis=1)
    y_pool = jnp.dot(dpool.astype(BF16), poolw_ref[...], preferred_element_type=F32) * pscale_ref[...]

    ext_ref[0:HALO, :] = ext_ref[ts:ts + HALO, :]

    zg = jax.nn.gelu(gmlp_in, approximate=True)
    u = zg[:, 0:dg]
    v = _layer_norm(zg[:, dg:2 * dg], glng_ref[...], glnb_ref[...]).astype(BF16)
    lane_g = lax.broadcasted_iota(jnp.int32, (GMLP_BLOCK, dg), 1)
    gw = gw_ref[...]
    gbias = gbias_ref[...]
    gates = []
    for nb in range(ts // GMLP_BLOCK):
        vb = v[nb * GMLP_BLOCK:(nb + 1) * GMLP_BLOCK, :]
        r = jnp.dot(gw, vb, preferred_element_type=F32)
        g = r[0:GMLP_BLOCK, :]
        for h in range(1, GMLP_HEADS):
            g = jnp.where(lane_g >= h * GMLP_HEAD_DIM, r[h * GMLP_BLOCK:(h + 1) * GMLP_BLOCK, :], g)
        gates.append(g + gbias)
    y_gmlp = u * jnp.concatenate(gates, axis=0)

    kv = kv_ref[0]
    mk = kv[:, 0:dg]
    mv = kv[:, dg:2 * dg]
    lane_q = lax.broadcasted_iota(jnp.int32, (ts, dg), 1)
    qs = mem_q * (MEM_HEAD_DIM ** -0.5)
    y_mem = jnp.zeros((ts, dg), F32)
    for h in range(MEM_HEADS):
        in_head = (lane_q >= h * MEM_HEAD_DIM) & (lane_q < (h + 1) * MEM_HEAD_DIM)
        qh = jnp.where(in_head, qs, 0.0).astype(BF16)
        sc = lax.dot_general(qh, mk, (((1,), (1,)), ((), ())), preferred_element_type=F32)
        e = jnp.exp(sc - jnp.max(sc, axis=-1, keepdims=True))
        den = jnp.sum(e, axis=-1, keepdims=True)
        oh = jnp.dot(e.astype(BF16), mv, preferred_element_type=F32) / den
        y_mem = jnp.where(in_head, oh, y_mem)

    gng = gng_ref[...]
    groups = []
    for gi, y in enumerate((y_conv, y_pool, y_gmlp, y_mem)):
        ms = jnp.mean(y * y, axis=-1, keepdims=True)
        groups.append(y * lax.rsqrt(ms + RMS_EPS) * gng[:, gi * dg:(gi + 1) * dg])
    cat = jnp.concatenate(groups, axis=1).astype(BF16)
    mix = jnp.dot(cat, wout_ref[...], preferred_element_type=F32)

    h1 = _layer_norm(DEEPNORM_ALPHA * x + mix, ln1g_ref[...], ln1b_ref[...])
    for j in range(ROW_TILE_ROWS):
        h1rt_ref[pl.ds(j, ts, stride=ROW_TILE_ROWS), :] = h1[:, j * LANES:(j + 1) * LANES]

    h1_hi = h1.astype(BF16)
    h1_lo = (h1 - h1_hi.astype(F32)).astype(BF16)
    logits = (jnp.dot(h1_hi, wrh_ref[...], preferred_element_type=F32)
              + jnp.dot(h1_lo, wrh_ref[...], preferred_element_type=F32)
              + jnp.dot(h1_hi, wrl_ref[...], preferred_element_type=F32)) + br_ref[...]
    lane = lax.broadcasted_iota(jnp.int32, (ts, LANES), 1)
    lane_f = lane.astype(F32)
    vals, idxs = [], []
    cur_l = logits
    for _ in range(TOP_K):
        m = jnp.max(cur_l, axis=-1, keepdims=True)
        i_f = jnp.min(jnp.where(cur_l == m, lane_f, float(LANES)), axis=-1, keepdims=True)
        vals.append(m)
        idxs.append(i_f)
        cur_l = jnp.where(lane_f == i_f, -jnp.inf, cur_l)
    es = [jnp.exp(vk - vals[0]) for vk in vals]
    den = es[0] + es[1] + es[2] + es[3]
    idx_out = jnp.zeros((ts, LANES), F32)
    gate_out = jnp.zeros((ts, LANES), F32)
    hits = jnp.zeros((ts, LANES), F32)
    for k in range(TOP_K):
        idx_out = jnp.where(lane == k, idxs[k], idx_out)
        gate_out = jnp.where(lane == k, es[k] / den, gate_out)
        hits = hits + jnp.where(lane_f == idxs[k], 1.0, 0.0)
    tok = (pl.program_id(0) * pl.num_programs(1) + s) * ts + lax.broadcasted_iota(jnp.int32, (ts, LANES), 0)
    key_ref[...] = jnp.where(lane < TOP_K, idx_out.astype(jnp.int32) * (1 << key_shift) + tok * TOP_K + lane, 0)
    gate_ref[...] = gate_out
    cnt_ref[0] = jnp.sum(hits, axis=0, keepdims=True)


def _mixer(h, kv_all, layer, lw, *, bsz, seq, ts, key_shift):
    n = bsz * seq
    st = seq // ts
    dg = D_GROUP
    const = lambda b, s: (0, 0)
    kern = functools.partial(_mixer_kernel, ts=ts, key_shift=key_shift)
    return pl.pallas_call(
        kern,
        grid=(bsz, st),
        in_specs=[
            pl.BlockSpec((ts, D_MODEL), lambda b, s: (b * st + s, 0)),
            pl.BlockSpec((1, kv_all.shape[1], 2 * dg), lambda b, s: (b, 0, layer)),
            pl.BlockSpec((D_MODEL, D_IN_PROJ), const),
            pl.BlockSpec((3, dg), const),
            pl.BlockSpec((dg, dg), const),
            pl.BlockSpec((1, dg), const),
            pl.BlockSpec((1, dg), const),
            pl.BlockSpec((1, dg), const),
            pl.BlockSpec((GMLP_HEADS * GMLP_BLOCK, GMLP_BLOCK), const),
            pl.BlockSpec((GMLP_BLOCK, dg), const),
            pl.BlockSpec((1, D_MODEL), const),
            pl.BlockSpec((D_MODEL, D_MODEL), const),
            pl.BlockSpec((1, D_MODEL), const),
            pl.BlockSpec((1, D_MODEL), const),
            pl.BlockSpec((D_MODEL, LANES), const),
            pl.BlockSpec((D_MODEL, LANES), const),
            pl.BlockSpec((1, LANES), const),
        ],
        out_specs=[
            pl.BlockSpec((ts * ROW_TILE_ROWS, LANES), lambda b, s: (b * st + s, 0)),
            pl.BlockSpec((ts, LANES), lambda b, s: (b * st + s, 0)),
            pl.BlockSpec((ts, LANES), lambda b, s: (b * st + s, 0)),
            pl.BlockSpec((1, 1, LANES), lambda b, s: (b * st + s, 0, 0)),
        ],
        out_shape=[
            jax.ShapeDtypeStruct((n * ROW_TILE_ROWS, LANES), F32),
            jax.ShapeDtypeStruct((n, LANES), jnp.int32),
            jax.ShapeDtypeStruct((n, LANES), F32),
            jax.ShapeDtypeStruct((bsz * st, 1, LANES), F32),
        ],
        scratch_shapes=[pltpu.VMEM((ts + HALO, 2 * dg), F32)],
        compiler_params=pltpu.CompilerParams(
            dimension_semantics=("arbitrary", "arbitrary"), vmem_limit_bytes=VMEM_LIMIT_BYTES),
        name="mixer",
    )(h, kv_all, *lw)


def _expert_kernel(te_ref, src0_ref, src_nxt_ref, dst_prev_ref, dst_last_ref, h_hbm, w1_ref, b1_ref, w2_ref, b2_ref,
                   ys_hbm, xbuf, obuf, hbuf, w1b, w2b, gsem, ssem, *, tm):
    i = pl.program_id(0)
    nt = pl.num_programs(0)
    slot = lax.rem(i, 2)
    rows = tm * ROW_TILE_ROWS

    def gather_row(src_ref, r, sl):
        t = src_ref[0, 0, r]
        return pltpu.make_async_copy(
            h_hbm.at[pl.ds(pl.multiple_of(t * ROW_TILE_ROWS, ROW_TILE_ROWS), ROW_TILE_ROWS), :],
            xbuf.at[sl, pl.ds(r * ROW_TILE_ROWS, ROW_TILE_ROWS), :], gsem.at[sl])

    def scatter_row(dst_ref, r, sl):
        t = dst_ref[0, 0, r]
        return pltpu.make_async_copy(
            obuf.at[sl, pl.ds(r * ROW_TILE_ROWS, ROW_TILE_ROWS), :],
            ys_hbm.at[pl.ds(pl.multiple_of(t * ROW_TILE_ROWS, ROW_TILE_ROWS), ROW_TILE_ROWS), :], ssem.at[sl])

    def wait_gather(sl):
        pltpu.make_async_copy(h_hbm.at[pl.ds(0, rows), :], xbuf.at[sl], gsem.at[sl]).wait()

    def wait_scatter(sl):
        pltpu.make_async_copy(obuf.at[sl], ys_hbm.at[pl.ds(0, rows), :], ssem.at[sl]).wait()

    @pl.when(i == 0)
    def _():
        obuf[1] = jnp.zeros((rows, LANES), F32)
        for r in range(tm):
            gather_row(src0_ref, r, 0).start()

    changed = jnp.logical_or(i == 0, te_ref[i] != te_ref[jnp.maximum(i - 1, 0)])

    @pl.when(changed)
    def _():
        w1b[...] = w1_ref[...].astype(BF16)
        w2b[...] = w2_ref[...].astype(BF16)

    n_chunks = EXPERT_CHUNKS
    cw = D_FF // n_chunks
    per = tm // n_chunks

    def step(sl):
        @pl.when(i >= 1)
        def _():
            wait_scatter(sl)

        wait_gather(sl)
        xs = xbuf.at[sl]
        x = jnp.concatenate([xs[pl.ds(j, tm, stride=ROW_TILE_ROWS), :] for j in range(ROW_TILE_ROWS)], axis=1)
        xb = x.astype(BF16)
        def row_dmas(lo, hi):
            for r in range(lo, hi):
                gather_row(src_nxt_ref, r, 1 - sl).start()
                scatter_row(dst_prev_ref, r, 1 - sl).start(priority=1)

        for c in range(2 * n_chunks):
            row_dmas(c * per // 4, (c + 1) * per // 4)
            hbuf[:, c * cw:(c + 1) * cw] = (jnp.dot(xb, w1b[:, c * cw:(c + 1) * cw], preferred_element_type=F32)
                                            + b1_ref[:, c * cw:(c + 1) * cw])
        out = None
        for c in range(n_chunks):
            row_dmas(tm // 2 + c * per // 2, tm // 2 + (c + 1) * per // 2)
            glu = jnp.minimum(hbuf[:, c * cw:(c + 1) * cw], SWIGLU_LIMIT)
            lin = jnp.clip(hbuf[:, D_FF + c * cw:D_FF + (c + 1) * cw], -SWIGLU_LIMIT, SWIGLU_LIMIT)
            act = glu * jax.nn.sigmoid(SWIGLU_ALPHA * glu) * (lin + 1.0)
            part = jnp.dot(act.astype(BF16), w2b[c * cw:(c + 1) * cw, :], preferred_element_type=F32)
            out = part if out is None else out + part
        out = out + b2_ref[...]

        os_ = obuf.at[sl]
        for j in range(ROW_TILE_ROWS):
            os_[pl.ds(j, tm, stride=ROW_TILE_ROWS), :] = out[:, j * LANES:(j + 1) * LANES]

        @pl.when(i == nt - 1)
        def _():
            for r in range(tm):
                scatter_row(dst_last_ref, r, sl).start(priority=1)
            wait_gather(1 - sl)
            wait_scatter(1 - sl)
            wait_scatter(sl)

    for sl in range(2):
        pl.when(slot == sl)(functools.partial(step, sl))


def _experts(h1rt, te, src, dstp, w1, b1, w2, b2, layer, *, tm, n_slots):
    nt = src.shape[0]
    rows = tm * ROW_TILE_ROWS
    kern = functools.partial(_expert_kernel, tm=tm)
    smem_blk = lambda f: pl.BlockSpec((1, 1, tm), f, memory_space=pltpu.SMEM)
    grid_spec = pltpu.PrefetchScalarGridSpec(
        num_scalar_prefetch=1,
        grid=(nt,),
        in_specs=[
            smem_blk(lambda i, te: (0, 0, 0)),
            smem_blk(lambda i, te: (jnp.minimum(i + 1, nt - 1), 0, 0)),
            smem_blk(lambda i, te: (i, 0, 0)),
            smem_blk(lambda i, te: (nt, 0, 0)),
            pl.BlockSpec(memory_space=pl.ANY),
            pl.BlockSpec((None, None, D_MODEL, 2 * D_FF), lambda i, te: (layer, te[i], 0, 0)),
            pl.BlockSpec((None, None, 1, 2 * D_FF), lambda i, te: (layer, te[i], 0, 0)),
            pl.BlockSpec((None, None, D_FF, D_MODEL), lambda i, te: (layer, te[i], 0, 0)),
            pl.BlockSpec((None, None, 1, D_MODEL), lambda i, te: (layer, te[i], 0, 0)),
        ],
        out_specs=pl.BlockSpec(memory_space=pl.ANY),
        scratch_shapes=[
            pltpu.VMEM((2, rows, LANES), F32),
            pltpu.VMEM((2, rows, LANES), F32),
            pltpu.VMEM((tm, 2 * D_FF), F32),
            pltpu.VMEM((D_MODEL, 2 * D_FF), BF16),
            pltpu.VMEM((D_FF, D_MODEL), BF16),
            pltpu.SemaphoreType.DMA((2,)),
            pltpu.SemaphoreType.DMA((2,)),
        ],
    )
    return pl.pallas_call(
        kern,
        grid_spec=grid_spec,
        out_shape=jax.ShapeDtypeStruct((n_slots * ROW_TILE_ROWS, LANES), F32),
        compiler_params=pltpu.CompilerParams(
            dimension_semantics=("arbitrary",), vmem_limit_bytes=VMEM_LIMIT_BYTES),
        name="experts",
    )(te, src, src, dstp, dstp, h1rt, w1, b1, w2, b2)


def _combine_kernel(y0_ref, y1_ref, y2_ref, y3_ref, h1rt_ref, gate_ref, g_ref, b_ref, o_ref, *, ts):
    gates = gate_ref[...]
    cols = []
    for j in range(ROW_TILE_ROWS):
        acc = DEEPNORM_ALPHA * h1rt_ref[pl.ds(j, ts, stride=ROW_TILE_ROWS), :]
        for k, y_ref in enumerate((y0_ref, y1_ref, y2_ref, y3_ref)):
            acc = acc + gates[:, k:k + 1] * y_ref[pl.ds(j, ts, stride=ROW_TILE_ROWS), :]
        cols.append(acc)
    r = jnp.concatenate(cols, axis=1)
    o_ref[...] = _layer_norm(r, g_ref[...], b_ref[...])


def _combine(ys, h1rt, gates, g, b, *, n, ts):
    nb = n // ts
    rows = ts * ROW_TILE_ROWS
    kern = functools.partial(_combine_kernel, ts=ts)
    y_spec = lambda k: pl.BlockSpec((rows, LANES), lambda i: (k * nb + i, 0))
    return pl.pallas_call(
        kern,
        grid=(nb,),
        in_specs=[y_spec(0), y_spec(1), y_spec(2), y_spec(3),
                  pl.BlockSpec((rows, LANES), lambda i: (i, 0)),
                  pl.BlockSpec((ts, LANES), lambda i: (i, 0)),
                  pl.BlockSpec((1, D_MODEL), lambda i: (0, 0)),
                  pl.BlockSpec((1, D_MODEL), lambda i: (0, 0))],
        out_specs=pl.BlockSpec((ts, D_MODEL), lambda i: (i, 0)),
        out_shape=jax.ShapeDtypeStruct((n, D_MODEL), F32),
        compiler_params=pltpu.CompilerParams(
            dimension_semantics=("arbitrary",), vmem_limit_bytes=VMEM_LIMIT_BYTES),
        name="combine",
    )(ys, ys, ys, ys, h1rt, gates, g.reshape(1, D_MODEL), b.reshape(1, D_MODEL))


def _key_shift(n, tm):
    return max((n * TOP_K - 1).bit_length(), (N_EXPERTS * tm - 1).bit_length()) + 1


def _routing_plan(keys, counts, n, tm):
    na = n * TOP_K
    n_tiles = na // tm + N_EXPERTS
    p_rows = n_tiles * tm
    shift = _key_shift(n, tm)
    flag = 1 << (shift - 1)
    need = (-counts) % tm
    e_ids = jnp.arange(N_EXPERTS, dtype=jnp.int32)[:, None]
    j_ids = jnp.arange(tm, dtype=jnp.int32)[None, :]
    pad_keys = jnp.where(j_ids < need[:, None], e_ids * (1 << shift) + flag + j_ids,
                         N_EXPERTS * (1 << shift) + flag + e_ids * tm + j_ids)
    ks = jnp.sort(jnp.concatenate([keys, pad_keys.reshape(-1)]))
    e_s = ks >> shift
    low = ks & (flag - 1)
    real = (ks & flag) == 0
    tok = low >> 2
    k = low & (TOP_K - 1)
    pad_id = jnp.where(e_s < N_EXPERTS, e_s * tm + low, low)
    src = jnp.where(real, tok, 0)
    dst = jnp.where(real, k * n + tok, na + pad_id)
    dummy = p_rows + jnp.arange(tm, dtype=jnp.int32)
    dstp = jnp.concatenate([dummy, dst]).reshape(n_tiles + 1, 1, tm)
    te = jnp.minimum(e_s[::tm], N_EXPERTS - 1)
    return te, src.reshape(n_tiles, 1, tm), dstp, p_rows + tm


def _pick_tile(total, want):
    t = min(total, want)
    while total % t:
        t //= 2
    return t


def kernel(x, mem, ln_in_g, ln_in_b, mem_ln_g, mem_ln_b, w_in, conv_w, pool_w, pool_scale, gmlp_ln_g, gmlp_ln_b,
           gmlp_w, gmlp_b, w_mem_kv, group_norm_g, w_out, ln1_g, ln1_b, w_router, b_router, w1, b1, w2, b2,
           ln2_g, ln2_b):
    bsz, seq, d = x.shape
    depth = w_in.shape[0]
    n = bsz * seq
    dg = D_GROUP
    ts = _pick_tile(seq, 512)
    tm = 256

    h = _ln_rows(x.reshape(n, d), ln_in_g, ln_in_b, _pick_tile(n, 1024))

    wkv_all = jnp.transpose(w_mem_kv, (1, 0, 2)).reshape(d, depth * 2 * dg).astype(BF16)
    kv_all = _mem_kv(mem, mem_ln_g, mem_ln_b, wkv_all)

    pos = jnp.arange(GMLP_BLOCK)
    mask = (pos[None, :] // CHUNK) <= (pos[:, None] // CHUNK)
    eye = jnp.eye(len(POOL_WINDOWS), dtype=F32)

    wr_pad = jnp.zeros((depth, d, LANES), F32).at[:, :, :N_EXPERTS].set(w_router)
    wr_hi = wr_pad.astype(BF16)
    wr_lo = (wr_pad - wr_hi.astype(F32)).astype(BF16)
    br_pad = jnp.full((depth, 1, LANES), NEG_BIG, F32).at[:, 0, :N_EXPERTS].set(b_router)
    b1r = b1.reshape(depth, N_EXPERTS, 1, 2 * D_FF)
    b2r = b2.reshape(depth, N_EXPERTS, 1, d)

    for l in range(depth):
        pool_bd = jnp.einsum('gh,gcd->gchd', eye, pool_w[l]).reshape(dg, dg).astype(BF16)
        gw = jnp.where(mask[None], gmlp_w[l], 0.0).reshape(GMLP_HEADS * GMLP_BLOCK, GMLP_BLOCK).astype(BF16)
        gbias = jnp.repeat(gmlp_b[l].T, GMLP_HEAD_DIM, axis=1)
        lw = (w_in[l].astype(BF16), conv_w[l], pool_bd, pool_scale[l].reshape(1, dg),
              gmlp_ln_g[l].reshape(1, dg), gmlp_ln_b[l].reshape(1, dg), gw, gbias,
              group_norm_g[l].reshape(1, d), w_out[l].astype(BF16),
              ln1_g[l].reshape(1, d), ln1_b[l].reshape(1, d), wr_hi[l], wr_lo[l], br_pad[l])
        h1rt, keys, gates, cnt = _mixer(h, kv_all, l, lw, bsz=bsz, seq=seq, ts=ts, key_shift=_key_shift(n, tm))
        counts = jnp.sum(cnt, axis=(0, 1))[:N_EXPERTS].astype(jnp.int32)
        te, src, dstp, n_slots = _routing_plan(keys[:, :TOP_K].reshape(-1), counts, n, tm)
        ys = _experts(h1rt, te, src, dstp, w1, b1r, w2, b2r, l, tm=tm, n_slots=n_slots)
        h = _combine(ys, h1rt, gates, ln2_g[l], ln2_b[l], n=n, ts=ts)
    return h.reshape(bsz, seq, d)
```

```python
import functools

import jax
import jax.numpy as jnp
from jax import lax
from jax.experimental import pallas as pl
from jax.experimental.pallas import tpu as pltpu

F32 = jnp.float32
BF16 = jnp.bfloat16

D_MODEL = 1024
N_GROUPS = 4
D_GROUP = D_MODEL // N_GROUPS
D_IN_PROJ = 7 * D_GROUP
POOL_WINDOWS = (2, 4, 8, 16)
POOL_CH = D_GROUP // len(POOL_WINDOWS)
GMLP_BLOCK = 128
GMLP_HEADS = 4
GMLP_HEAD_DIM = D_GROUP // GMLP_HEADS
CHUNK = 64
MEM_HEADS = 4
MEM_HEAD_DIM = D_GROUP // MEM_HEADS
N_EXPERTS = 32
TOP_K = 4
D_FF = D_MODEL
SWIGLU_ALPHA = 1.702
SWIGLU_LIMIT = 7.0
REF_DEPTH = 4
DEEPNORM_ALPHA = (2 * REF_DEPTH) ** 0.25
LN_EPS = 1e-5
RMS_EPS = 1e-6

SUBLANES = 8
LANES = 128
ROW_TILE_ROWS = D_MODEL // LANES
HALO = max(POOL_WINDOWS)
VMEM_LIMIT_BYTES = 58 * 1024 * 1024
EXPERT_CHUNKS = 4
EXPERT_TILE = 256
EXPERT_BLOCK_TOKENS = 4096
STAGE_PITCH = EXPERT_TILE + SUBLANES
FINAL_CHUNK = 128
ROW_BATCH = 8

NEG_BIG = -1e30


def _layer_norm(x, g, b):
    mu = jnp.mean(x, axis=-1, keepdims=True)
    xc = x - mu
    var = jnp.mean(xc * xc, axis=-1, keepdims=True)
    return xc * lax.rsqrt(var + LN_EPS) * g + b


def _ln_kernel(x_ref, g_ref, b_ref, o_ref, *, rows):
    y = _layer_norm(x_ref[...], g_ref[...], b_ref[...])
    for j in range(ROW_TILE_ROWS):
        o_ref[pl.ds(j, rows, stride=ROW_TILE_ROWS), :] = y[:, j * LANES:(j + 1) * LANES]


def _ln_rows(x, g, b, rows):
    n, d = x.shape
    return pl.pallas_call(
        functools.partial(_ln_kernel, rows=rows),
        grid=(n // rows,),
        in_specs=[pl.BlockSpec((rows, d), lambda i: (i, 0)),
                  pl.BlockSpec((1, d), lambda i: (0, 0)),
                  pl.BlockSpec((1, d), lambda i: (0, 0))],
        out_specs=pl.BlockSpec((rows * ROW_TILE_ROWS, LANES), lambda i: (i, 0)),
        out_shape=jax.ShapeDtypeStruct((n * ROW_TILE_ROWS, LANES), F32),
        name="entry_ln",
    )(x, g.reshape(1, d), b.reshape(1, d))


def _mem_kv_kernel(m_ref, g_ref, b_ref, w_ref, o_ref):
    mn = _layer_norm(m_ref[0], g_ref[...], b_ref[...])
    o_ref[0] = jnp.dot(mn.astype(BF16), w_ref[...], preferred_element_type=F32).astype(BF16)


def _mem_kv(mem, g, b, w_all):
    bsz, m, d = mem.shape
    p = w_all.shape[1]
    return pl.pallas_call(
        _mem_kv_kernel,
        grid=(bsz,),
        in_specs=[pl.BlockSpec((1, m, d), lambda i: (i, 0, 0)),
                  pl.BlockSpec((1, d), lambda i: (0, 0)),
                  pl.BlockSpec((1, d), lambda i: (0, 0)),
                  pl.BlockSpec((d, p), lambda i: (0, 0))],
        out_specs=pl.BlockSpec((1, m, p), lambda i: (i, 0, 0)),
        out_shape=jax.ShapeDtypeStruct((bsz, m, p), BF16),
        name="mem_kv",
    )(mem, g.reshape(1, d), b.reshape(1, d), w_all)


def _mixer_kernel(x_ref, kv_ref, win_ref, convw_ref, poolw_ref, pscale_ref, glng_ref, glnb_ref,
                  gw_ref, gbias_ref, gng_ref, wout_ref, ln1g_ref, ln1b_ref, wrh_ref, wrl_ref, br_ref,
                  h1rt_ref, key_ref, gate_ref, cnt_ref, ext_ref, *, ts, key_shift, block_shift):
    s = pl.program_id(1)
    dg = D_GROUP

    x = jnp.concatenate([x_ref[pl.ds(j, ts, stride=ROW_TILE_ROWS), :] for j in range(ROW_TILE_ROWS)], axis=1)
    proj = jnp.dot(x.astype(BF16), win_ref[...], preferred_element_type=F32)
    gate_b = proj[:, 0:dg]
    gate_c = proj[:, dg:2 * dg]
    conv_in = proj[:, 2 * dg:3 * dg]
    pool_in = proj[:, 3 * dg:4 * dg]
    gmlp_in = proj[:, 4 * dg:6 * dg]
    mem_q = proj[:, 6 * dg:7 * dg]

    @pl.when(s == 0)
    def _():
        ext_ref[0:HALO, :] = jnp.zeros((HALO, 2 * dg), F32)

    z = gate_c * conv_in
    ext_ref[HALO:HALO + ts, 0:dg] = z
    ext_ref[HALO:HALO + ts, dg:2 * dg] = pool_in

    zm1 = ext_ref[pl.ds(HALO - 1, ts), 0:dg]
    zm2 = ext_ref[pl.ds(HALO - 2, ts), 0:dg]
    cw = convw_ref[...]
    y_conv = gate_b * (cw[0:1, :] * zm2 + cw[1:2, :] * zm1 + cw[2:3, :] * z)

    pos = s * ts + lax.broadcasted_iota(jnp.int32, (ts, 1), 0)
    half = dg // 2
    lane_h = lax.broadcasted_iota(jnp.int32, (ts, half), 1)
    diffs = []
    for hf in range(2):
        c0 = dg + hf * half
        w_lo, w_hi = POOL_WINDOWS[2 * hf], POOL_WINDOWS[2 * hf + 1]
        cur = ext_ref[pl.ds(HALO, ts), c0:c0 + half]
        acc = cur
        for k in range(1, w_lo):
            acc = acc + ext_ref[pl.ds(HALO - k, ts), c0:c0 + half]
        s_lo = acc
        for k in range(w_lo, w_hi):
            acc = acc + ext_ref[pl.ds(HALO - k, ts), c0:c0 + half]
        s_hi = acc
        cnt_lo = jnp.minimum(pos + 1, w_lo).astype(F32)
        cnt_hi = jnp.minimum(pos + 1, w_hi).astype(F32)
        mean = jnp.where(lane_h < POOL_CH, s_lo / cnt_lo, s_hi / cnt_hi)
        diffs.append(mean - cur)
    dpool = jnp.concatenate(diffs, axis=1)
    y_pool = jnp.dot(dpool.astype(BF16), poolw_ref[...], preferred_element_type=F32) * pscale_ref[...]

    ext_ref[0:HALO, :] = ext_ref[ts:ts + HALO, :]

    zg = jax.nn.gelu(gmlp_in, approximate=True)
    u = zg[:, 0:dg]
    v = _layer_norm(zg[:, dg:2 * dg], glng_ref[...], glnb_ref[...]).astype(BF16)
    lane_g = lax.broadcasted_iota(jnp.int32, (GMLP_BLOCK, dg), 1)
    gw = gw_ref[...]
    gbias = gbias_ref[...]
    gates = []
    for nb in range(ts // GMLP_BLOCK):
        vb = v[nb * GMLP_BLOCK:(nb + 1) * GMLP_BLOCK, :]
        r = jnp.dot(gw, vb, preferred_element_type=F32)
        g = r[0:GMLP_BLOCK, :]
        for h in range(1, GMLP_HEADS):
            g = jnp.where(lane_g >= h * GMLP_HEAD_DIM, r[h * GMLP_BLOCK:(h + 1) * GMLP_BLOCK, :], g)
        gates.append(g + gbias)
    y_gmlp = u * jnp.concatenate(gates, axis=0)

    kv = kv_ref[0]
    mk = kv[:, 0:dg]
    mv = kv[:, dg:2 * dg]
    lane_q = lax.broadcasted_iota(jnp.int32, (ts, dg), 1)
    qs = mem_q * (MEM_HEAD_DIM ** -0.5)
    y_mem = jnp.zeros((ts, dg), F32)
    for h in range(MEM_HEADS):
        in_head = (lane_q >= h * MEM_HEAD_DIM) & (lane_q < (h + 1) * MEM_HEAD_DIM)
        qh = jnp.where(in_head, qs, 0.0).astype(BF16)
        sc = lax.dot_general(qh, mk, (((1,), (1,)), ((), ())), preferred_element_type=F32)
        e = jnp.exp(sc - jnp.max(sc, axis=-1, keepdims=True))
        den = jnp.sum(e, axis=-1, keepdims=True)
        oh = jnp.dot(e.astype(BF16), mv, preferred_element_type=F32) / den
        y_mem = jnp.where(in_head, oh, y_mem)

    gng = gng_ref[...]
    groups = []
    for gi, y in enumerate((y_conv, y_pool, y_gmlp, y_mem)):
        ms = jnp.mean(y * y, axis=-1, keepdims=True)
        groups.append(y * lax.rsqrt(ms + RMS_EPS) * gng[:, gi * dg:(gi + 1) * dg])
    cat = jnp.concatenate(groups, axis=1).astype(BF16)
    mix = jnp.dot(cat, wout_ref[...], preferred_element_type=F32)

    h1 = _layer_norm(DEEPNORM_ALPHA * x + mix, ln1g_ref[...], ln1b_ref[...])
    for j in range(ROW_TILE_ROWS):
        h1rt_ref[pl.ds(j, ts, stride=ROW_TILE_ROWS), :] = h1[:, j * LANES:(j + 1) * LANES]

    h1_hi = h1.astype(BF16)
    h1_lo = (h1 - h1_hi.astype(F32)).astype(BF16)
    logits = (jnp.dot(h1_hi, wrh_ref[...], preferred_element_type=F32)
              + jnp.dot(h1_lo, wrh_ref[...], preferred_element_type=F32)
              + jnp.dot(h1_hi, wrl_ref[...], preferred_element_type=F32)) + br_ref[...]
    lane = lax.broadcasted_iota(jnp.int32, (ts, LANES), 1)
    lane_f = lane.astype(F32)
    vals, idxs = [], []
    cur_l = logits
    for _ in range(TOP_K):
        m = jnp.max(cur_l, axis=-1, keepdims=True)
        i_f = jnp.min(jnp.where(cur_l == m, lane_f, float(LANES)), axis=-1, keepdims=True)
        vals.append(m)
        idxs.append(i_f)
        cur_l = jnp.where(lane_f == i_f, -jnp.inf, cur_l)
    es = [jnp.exp(vk - vals[0]) for vk in vals]
    den = es[0] + es[1] + es[2] + es[3]
    idx_out = jnp.zeros((ts, LANES), F32)
    gate_out = jnp.zeros((ts, LANES), F32)
    hits = jnp.zeros((ts, LANES), F32)
    for k in range(TOP_K):
        idx_out = jnp.where(lane == k, idxs[k], idx_out)
        gate_out = jnp.where(lane == k, es[k] / den, gate_out)
        hits = hits + jnp.where(lane_f == idxs[k], 1.0, 0.0)
    tok = (pl.program_id(0) * pl.num_programs(1) + s) * ts + lax.broadcasted_iota(jnp.int32, (ts, LANES), 0)
    group = (tok >> block_shift) * N_EXPERTS + idx_out.astype(jnp.int32)
    key_ref[...] = jnp.where(lane < TOP_K, group * (1 << key_shift) + tok * TOP_K + lane, 0)
    gate_ref[...] = gate_out
    cnt_ref[0] = jnp.sum(hits, axis=0, keepdims=True)


def _mixer(h, kv_all, layer, lw, *, bsz, seq, ts, key_shift, block_shift):
    n = bsz * seq
    st = seq // ts
    dg = D_GROUP
    const = lambda b, s: (0, 0)
    kern = functools.partial(_mixer_kernel, ts=ts, key_shift=key_shift, block_shift=block_shift)
    return pl.pallas_call(
        kern,
        grid=(bsz, st),
        in_specs=[
            pl.BlockSpec((ts * ROW_TILE_ROWS, LANES), lambda b, s: (b * st + s, 0)),
            pl.BlockSpec((1, kv_all.shape[1], 2 * dg), lambda b, s: (b, 0, layer)),
            pl.BlockSpec((D_MODEL, D_IN_PROJ), const),
            pl.BlockSpec((3, dg), const),
            pl.BlockSpec((dg, dg), const),
            pl.BlockSpec((1, dg), const),
            pl.BlockSpec((1, dg), const),
            pl.BlockSpec((1, dg), const),
            pl.BlockSpec((GMLP_HEADS * GMLP_BLOCK, GMLP_BLOCK), const),
            pl.BlockSpec((GMLP_BLOCK, dg), const),
            pl.BlockSpec((1, D_MODEL), const),
            pl.BlockSpec((D_MODEL, D_MODEL), const),
            pl.BlockSpec((1, D_MODEL), const),
            pl.BlockSpec((1, D_MODEL), const),
            pl.BlockSpec((D_MODEL, LANES), const),
            pl.BlockSpec((D_MODEL, LANES), const),
            pl.BlockSpec((1, LANES), const),
        ],
        out_specs=[
            pl.BlockSpec((ts * ROW_TILE_ROWS, LANES), lambda b, s: (b * st + s, 0)),
            pl.BlockSpec((ts, LANES), lambda b, s: (b * st + s, 0)),
            pl.BlockSpec((ts, LANES), lambda b, s: (b * st + s, 0)),
            pl.BlockSpec((1, 1, LANES), lambda b, s: (b * st + s, 0, 0)),
        ],
        out_shape=[
            jax.ShapeDtypeStruct((n * ROW_TILE_ROWS, LANES), F32),
            jax.ShapeDtypeStruct((n, LANES), jnp.int32),
            jax.ShapeDtypeStruct((n, LANES), F32),
            jax.ShapeDtypeStruct((bsz * st, 1, LANES), F32),
        ],
        scratch_shapes=[pltpu.VMEM((ts + HALO, 2 * dg), F32)],
        compiler_params=pltpu.CompilerParams(
            dimension_semantics=("arbitrary", "arbitrary"), vmem_limit_bytes=VMEM_LIMIT_BYTES),
        name="mixer",
    )(h, kv_all, *lw)


def _expert_kernel(te_ref, tb_ref, nact_ref, idx_ref, gate_ref, h_hbm, w1_ref, b1_ref, w2_ref, b2_ref, g_ref, bt_ref,
                   o_hbm, xblk, ybuf, stage, hbuf, xsem, osem, *, tm, tbt):
    i = pl.program_id(0)
    nt = pl.num_programs(0)
    n_act = nact_ref[0]
    rt = ROW_TILE_ROWS
    blk_rows = tbt * rt
    sp = STAGE_PITCH
    cw = D_FF // EXPERT_CHUNKS

    @pl.when(i == 0)
    def _():
        xblk[pl.ds(blk_rows, rt), :] = jnp.zeros((rt, LANES), F32)

    @pl.when(i < n_act)
    def _():
        b = tb_ref[i]
        first = jnp.logical_or(i == 0, tb_ref[jnp.maximum(i - 1, 0)] != b)
        last = jnp.logical_or(i == n_act - 1, tb_ref[jnp.minimum(i + 1, nt - 1)] != b)
        row0 = pl.multiple_of(b * blk_rows, blk_rows)

        @pl.when(first)
        def _():
            load = pltpu.make_async_copy(h_hbm.at[pl.ds(row0, blk_rows), :], xblk.at[pl.ds(0, blk_rows), :], xsem)
            load.start()
            zrows = FINAL_CHUNK * rt

            def zero(c, carry):
                ybuf[pl.ds(pl.multiple_of(c * zrows, zrows), zrows), :] = jnp.zeros((zrows, LANES), F32)
                return carry

            lax.fori_loop(0, tbt // FINAL_CHUNK, zero, 0)
            ybuf[pl.ds(blk_rows, rt), :] = jnp.zeros((rt, LANES), F32)
            load.wait()

        toks = [idx_ref[0, 0, r] for r in range(tm)]
        for r in range(tm):
            stage[pl.ds(r, rt, stride=sp), :] = xblk[pl.ds(pl.multiple_of(toks[r] * rt, rt), rt), :]
        x = jnp.concatenate([stage[pl.ds(j * sp, tm), :] for j in range(rt)], axis=1)
        xb = x.astype(BF16)

        for c in range(2 * EXPERT_CHUNKS):
            hbuf[:, c * cw:(c + 1) * cw] = (jnp.dot(xb, w1_ref[:, c * cw:(c + 1) * cw], preferred_element_type=F32)
                                            + b1_ref[:, c * cw:(c + 1) * cw])
        out = None
        for c in range(EXPERT_CHUNKS):
            glu = jnp.minimum(hbuf[:, c * cw:(c + 1) * cw], SWIGLU_LIMIT)
            lin = jnp.clip(hbuf[:, D_FF + c * cw:D_FF + (c + 1) * cw], -SWIGLU_LIMIT, SWIGLU_LIMIT)
            act = glu * jax.nn.sigmoid(SWIGLU_ALPHA * glu) * (lin + 1.0)
            part = jnp.dot(act.astype(BF16), w2_ref[c * cw:(c + 1) * cw, :], preferred_element_type=F32)
            out = part if out is None else out + part
        out = out + b2_ref[...]
        for j in range(rt):
            stage[pl.ds(j * sp, tm), :] = out[:, j * LANES:(j + 1) * LANES]

        for r0 in range(0, tm, ROW_BATCH):
            rs = range(r0, r0 + ROW_BATCH)
            accs = [ybuf[pl.ds(pl.multiple_of(toks[r] * rt, rt), rt), :] for r in rs]
            rows = [stage[pl.ds(r, rt, stride=sp), :] for r in rs]
            for k, r in enumerate(rs):
                ybuf[pl.ds(pl.multiple_of(toks[r] * rt, rt), rt), :] = accs[k] + gate_ref[0, 0, r] * rows[k]

        @pl.when(last)
        def _():
            crows = FINAL_CHUNK * rt
            g3 = g_ref[...].reshape(1, rt, LANES)
            b3 = bt_ref[...].reshape(1, rt, LANES)

            def finish(c, carry):
                rows_c = pl.ds(pl.multiple_of(c * crows, crows), crows)
                r3 = (DEEPNORM_ALPHA * xblk[rows_c, :] + ybuf[rows_c, :]).reshape(FINAL_CHUNK, rt, LANES)
                mu = jnp.sum(jnp.sum(r3, axis=2, keepdims=True), axis=1, keepdims=True) * (1.0 / D_MODEL)
                xc = r3 - mu
                var = jnp.sum(jnp.sum(xc * xc, axis=2, keepdims=True), axis=1, keepdims=True) * (1.0 / D_MODEL)
                y3 = xc * lax.rsqrt(var + LN_EPS) * g3 + b3
                ybuf[rows_c, :] = y3.reshape(crows, LANES)
                pltpu.make_async_copy(
                    ybuf.at[rows_c, :],
                    o_hbm.at[pl.ds(pl.multiple_of(row0 + c * crows, crows), crows), :], osem).start()
                return carry

            lax.fori_loop(0, tbt // FINAL_CHUNK, finish, 0)
            pltpu.make_async_copy(ybuf.at[pl.ds(0, blk_rows), :], o_hbm.at[pl.ds(row0, blk_rows), :], osem).wait()


def _experts(h1rt, te, tb, nact, idx, gate, w1b, b1, w2b, b2, ln_g, ln_b, layer, *, tm, tbt):
    nt = idx.shape[0]
    n_rows = h1rt.shape[0]
    kern = functools.partial(_expert_kernel, tm=tm, tbt=tbt)
    smem_blk = pl.BlockSpec((1, 1, tm), lambda i, te, tb, na: (i, 0, 0), memory_space=pltpu.SMEM)
    per_expert = lambda i, te, tb, na: (layer, te[i], 0, 0)
    const = lambda i, te, tb, na: (0, 0)
    grid_spec = pltpu.PrefetchScalarGridSpec(
        num_scalar_prefetch=3,
        grid=(nt,),
        in_specs=[
            smem_blk,
            smem_blk,
            pl.BlockSpec(memory_space=pl.ANY),
            pl.BlockSpec((None, None, D_MODEL, 2 * D_FF), per_expert),
            pl.BlockSpec((None, None, 1, 2 * D_FF), per_expert),
            pl.BlockSpec((None, None, D_FF, D_MODEL), per_expert),
            pl.BlockSpec((None, None, 1, D_MODEL), per_expert),
            pl.BlockSpec((ROW_TILE_ROWS, LANES), const),
            pl.BlockSpec((ROW_TILE_ROWS, LANES), const),
        ],
        out_specs=pl.BlockSpec(memory_space=pl.ANY),
        scratch_shapes=[
            pltpu.VMEM(((tbt + 1) * ROW_TILE_ROWS, LANES), F32),
            pltpu.VMEM(((tbt + 1) * ROW_TILE_ROWS, LANES), F32),
            pltpu.VMEM((ROW_TILE_ROWS * STAGE_PITCH, LANES), F32),
            pltpu.VMEM((tm, 2 * D_FF), F32),
            pltpu.SemaphoreType.DMA,
            pltpu.SemaphoreType.DMA,
        ],
    )
    return pl.pallas_call(
        kern,
        grid_spec=grid_spec,
        out_shape=jax.ShapeDtypeStruct((n_rows, LANES), F32),
        compiler_params=pltpu.CompilerParams(
            dimension_semantics=("arbitrary",), vmem_limit_bytes=VMEM_LIMIT_BYTES),
        name="experts",
    )(te, tb, nact, idx, gate, h1rt, w1b, b1, w2b, b2, ln_g, ln_b)


def _key_shift(n, n_groups, tm):
    return max((n * TOP_K - 1).bit_length(), (n_groups * tm - 1).bit_length()) + 1


def _routing_plan(keys, gates_flat, counts, n, tm, tbt):
    nb = n // tbt
    n_groups = nb * N_EXPERTS
    na = n * TOP_K
    n_tiles = na // tm + n_groups
    shift = _key_shift(n, n_groups, tm)
    flag = 1 << (shift - 1)
    need = ((-counts) % tm).reshape(n_groups, 1)
    g_ids = jnp.arange(n_groups, dtype=jnp.int32)[:, None]
    j_ids = jnp.arange(tm, dtype=jnp.int32)[None, :]
    pad_keys = jnp.where(j_ids < need, g_ids * (1 << shift) + flag + j_ids,
                         n_groups * (1 << shift) + flag + g_ids * tm + j_ids)
    ks = jnp.sort(jnp.concatenate([keys, pad_keys.reshape(-1)]))
    low = ks & (flag - 1)
    real = (ks & flag) == 0
    idx = jnp.where(real, (low >> 2) & (tbt - 1), tbt)
    gate = jnp.where(real, gates_flat[jnp.where(real, low, 0)], 0.0)
    g_t = ks[::tm] >> shift
    active = g_t < n_groups
    te = jnp.where(active, g_t & (N_EXPERTS - 1), N_EXPERTS - 1)
    tb = jnp.minimum(g_t >> 5, nb - 1)
    nact = jnp.sum(active.astype(jnp.int32)).reshape(1)
    return te, tb, nact, idx.reshape(n_tiles, 1, tm), gate.reshape(n_tiles, 1, tm)


def _pick_tile(total, want):
    t = min(total, want)
    while total % t:
        t //= 2
    return t


def kernel(x, mem, ln_in_g, ln_in_b, mem_ln_g, mem_ln_b, w_in, conv_w, pool_w, pool_scale, gmlp_ln_g, gmlp_ln_b,
           gmlp_w, gmlp_b, w_mem_kv, group_norm_g, w_out, ln1_g, ln1_b, w_router, b_router, w1, b1, w2, b2,
           ln2_g, ln2_b):
    bsz, seq, d = x.shape
    depth = w_in.shape[0]
    n = bsz * seq
    dg = D_GROUP
    ts = _pick_tile(seq, 512)
    tm = EXPERT_TILE
    tbt = _pick_tile(n, EXPERT_BLOCK_TOKENS)
    nb = n // tbt
    assert N_EXPERTS == 32 and tbt % ts == 0 and tbt & (tbt - 1) == 0 and tbt % FINAL_CHUNK == 0
    key_shift = _key_shift(n, nb * N_EXPERTS, tm)
    block_shift = tbt.bit_length() - 1

    h = _ln_rows(x.reshape(n, d), ln_in_g, ln_in_b, _pick_tile(n, 1024))

    wkv_all = jnp.transpose(w_mem_kv, (1, 0, 2)).reshape(d, depth * 2 * dg).astype(BF16)
    kv_all = _mem_kv(mem, mem_ln_g, mem_ln_b, wkv_all)

    pos = jnp.arange(GMLP_BLOCK)
    mask = (pos[None, :] // CHUNK) <= (pos[:, None] // CHUNK)
    eye = jnp.eye(len(POOL_WINDOWS), dtype=F32)

    wr_pad = jnp.zeros((depth, d, LANES), F32).at[:, :, :N_EXPERTS].set(w_router)
    wr_hi = wr_pad.astype(BF16)
    wr_lo = (wr_pad - wr_hi.astype(F32)).astype(BF16)
    br_pad = jnp.full((depth, 1, LANES), NEG_BIG, F32).at[:, 0, :N_EXPERTS].set(b_router)
    w1b = w1.astype(BF16)
    w2b = w2.astype(BF16)
    b1r = b1.reshape(depth, N_EXPERTS, 1, 2 * D_FF)
    b2r = b2.reshape(depth, N_EXPERTS, 1, d)

    for l in range(depth):
        pool_bd = jnp.einsum('gh,gcd->gchd', eye, pool_w[l]).reshape(dg, dg).astype(BF16)
        gw = jnp.where(mask[None], gmlp_w[l], 0.0).reshape(GMLP_HEADS * GMLP_BLOCK, GMLP_BLOCK).astype(BF16)
        gbias = jnp.repeat(gmlp_b[l].T, GMLP_HEAD_DIM, axis=1)
        lw = (w_in[l].astype(BF16), conv_w[l], pool_bd, pool_scale[l].reshape(1, dg),
              gmlp_ln_g[l].reshape(1, dg), gmlp_ln_b[l].reshape(1, dg), gw, gbias,
              group_norm_g[l].reshape(1, d), w_out[l].astype(BF16),
              ln1_g[l].reshape(1, d), ln1_b[l].reshape(1, d), wr_hi[l], wr_lo[l], br_pad[l])
        h1rt, keys, gates, cnt = _mixer(h, kv_all, l, lw, bsz=bsz, seq=seq, ts=ts,
                                        key_shift=key_shift, block_shift=block_shift)
        counts = jnp.sum(cnt.reshape(nb, tbt // ts, LANES), axis=1)[:, :N_EXPERTS].astype(jnp.int32)
        te, tb, nact, idx, gate = _routing_plan(keys[:, :TOP_K].reshape(-1), gates[:, :TOP_K].reshape(-1),
                                                counts, n, tm, tbt)
        h = _experts(h1rt, te, tb, nact, idx, gate, w1b, b1r, w2b, b2r,
                     ln2_g[l].reshape(ROW_TILE_ROWS, LANES), ln2_b[l].reshape(ROW_TILE_ROWS, LANES), l,
                     tm=tm, tbt=tbt)
    return h.reshape(bsz, seq, d)
```

```python
import functools

import jax
import jax.numpy as jnp
from jax import lax
from jax.experimental import pallas as pl
from jax.experimental.pallas import tpu as pltpu

F32 = jnp.float32
BF16 = jnp.bfloat16

D_MODEL = 1024
N_GROUPS = 4
D_GROUP = D_MODEL // N_GROUPS
D_IN_PROJ = 7 * D_GROUP
POOL_WINDOWS = (2, 4, 8, 16)
POOL_CH = D_GROUP // len(POOL_WINDOWS)
GMLP_BLOCK = 128
GMLP_HEADS = 4
GMLP_HEAD_DIM = D_GROUP // GMLP_HEADS
CHUNK = 64
MEM_HEADS = 4
MEM_HEAD_DIM = D_GROUP // MEM_HEADS
N_EXPERTS = 32
TOP_K = 4
D_FF = D_MODEL
SWIGLU_ALPHA = 1.702
SWIGLU_LIMIT = 7.0
REF_DEPTH = 4
DEEPNORM_ALPHA = (2 * REF_DEPTH) ** 0.25
LN_EPS = 1e-5
RMS_EPS = 1e-6

SUBLANES = 8
LANES = 128
ROW_TILE_ROWS = D_MODEL // LANES
HALO = max(POOL_WINDOWS)
VMEM_LIMIT_BYTES = 58 * 1024 * 1024
EXPERT_CHUNKS = 4
EXPERT_TILE = 256
EXPERT_BLOCK_TOKENS = 4096
STAGE_PITCH = EXPERT_TILE + SUBLANES
FINAL_CHUNK = 128
ROW_BATCH = 8

NEG_BIG = -1e30


def _layer_norm(x, g, b):
    mu = jnp.mean(x, axis=-1, keepdims=True)
    xc = x - mu
    var = jnp.mean(xc * xc, axis=-1, keepdims=True)
    return xc * lax.rsqrt(var + LN_EPS) * g + b


def _ln_kernel(x_ref, g_ref, b_ref, o_ref, *, rows):
    y = _layer_norm(x_ref[...], g_ref[...], b_ref[...])
    for j in range(ROW_TILE_ROWS):
        o_ref[pl.ds(j, rows, stride=ROW_TILE_ROWS), :] = y[:, j * LANES:(j + 1) * LANES]


def _ln_rows(x, g, b, rows):
    n, d = x.shape
    return pl.pallas_call(
        functools.partial(_ln_kernel, rows=rows),
        grid=(n // rows,),
        in_specs=[pl.BlockSpec((rows, d), lambda i: (i, 0)),
                  pl.BlockSpec((1, d), lambda i: (0, 0)),
                  pl.BlockSpec((1, d), lambda i: (0, 0))],
        out_specs=pl.BlockSpec((rows * ROW_TILE_ROWS, LANES), lambda i: (i, 0)),
        out_shape=jax.ShapeDtypeStruct((n * ROW_TILE_ROWS, LANES), F32),
        name="entry_ln",
    )(x, g.reshape(1, d), b.reshape(1, d))


def _mem_kv_kernel(m_ref, g_ref, b_ref, w_ref, o_ref):
    mn = _layer_norm(m_ref[0], g_ref[...], b_ref[...])
    o_ref[0] = jnp.dot(mn.astype(BF16), w_ref[...], preferred_element_type=F32).astype(BF16)


def _mem_kv(mem, g, b, w_all):
    bsz, m, d = mem.shape
    p = w_all.shape[1]
    return pl.pallas_call(
        _mem_kv_kernel,
        grid=(bsz,),
        in_specs=[pl.BlockSpec((1, m, d), lambda i: (i, 0, 0)),
                  pl.BlockSpec((1, d), lambda i: (0, 0)),
                  pl.BlockSpec((1, d), lambda i: (0, 0)),
                  pl.BlockSpec((d, p), lambda i: (0, 0))],
        out_specs=pl.BlockSpec((1, m, p), lambda i: (i, 0, 0)),
        out_shape=jax.ShapeDtypeStruct((bsz, m, p), BF16),
        name="mem_kv",
    )(mem, g.reshape(1, d), b.reshape(1, d), w_all)


def _mixer_kernel(x_ref, kv_ref, win_ref, convw_ref, poolw_ref, pscale_ref, glng_ref, glnb_ref,
                  gw_ref, gbias_ref, gng_ref, wout_ref, ln1g_ref, ln1b_ref, wrh_ref, wrl_ref, br_ref,
                  h1rt_ref, key_ref, gate_ref, cnt_ref, ext_ref, *, ts, key_shift, block_shift):
    s = pl.program_id(1)
    dg = D_GROUP

    x = jnp.concatenate([x_ref[pl.ds(j, ts, stride=ROW_TILE_ROWS), :] for j in range(ROW_TILE_ROWS)], axis=1)
    proj = jnp.dot(x.astype(BF16), win_ref[...], preferred_element_type=F32)
    gate_b = proj[:, 0:dg]
    gate_c = proj[:, dg:2 * dg]
    conv_in = proj[:, 2 * dg:3 * dg]
    pool_in = proj[:, 3 * dg:4 * dg]
    gmlp_in = proj[:, 4 * dg:6 * dg]
    mem_q = proj[:, 6 * dg:7 * dg]

    @pl.when(s == 0)
    def _():
        ext_ref[0:HALO, :] = jnp.zeros((HALO, 2 * dg), F32)

    z = gate_c * conv_in
    ext_ref[HALO:HALO + ts, 0:dg] = z
    ext_ref[HALO:HALO + ts, dg:2 * dg] = pool_in

    zm1 = ext_ref[pl.ds(HALO - 1, ts), 0:dg]
    zm2 = ext_ref[pl.ds(HALO - 2, ts), 0:dg]
    cw = convw_ref[...]
    y_conv = gate_b * (cw[0:1, :] * zm2 + cw[1:2, :] * zm1 + cw[2:3, :] * z)

    pos = s * ts + lax.broadcasted_iota(jnp.int32, (ts, 1), 0)
    half = dg // 2
    lane_h = lax.broadcasted_iota(jnp.int32, (ts, half), 1)
    diffs = []
    for hf in range(2):
        c0 = dg + hf * half
        w_lo, w_hi = POOL_WINDOWS[2 * hf], POOL_WINDOWS[2 * hf + 1]
        cur = ext_ref[pl.ds(HALO, ts), c0:c0 + half]
        acc = cur
        for k in range(1, w_lo):
            acc = acc + ext_ref[pl.ds(HALO - k, ts), c0:c0 + half]
        s_lo = acc
        for k in range(w_lo, w_hi):
            acc = acc + ext_ref[pl.ds(HALO - k, ts), c0:c0 + half]
        s_hi = acc
        cnt_lo = jnp.minimum(pos + 1, w_lo).astype(F32)
        cnt_hi = jnp.minimum(pos + 1, w_hi).astype(F32)
        mean = jnp.where(lane_h < POOL_CH, s_lo / cnt_lo, s_hi / cnt_hi)
        diffs.append(mean - cur)
    dpool = jnp.concatenate(diffs, axis=1)
    y_pool = jnp.dot(dpool.astype(BF16), poolw_ref[...], preferred_element_type=F32) * pscale_ref[...]

    ext_ref[0:HALO, :] = ext_ref[ts:ts + HALO, :]

    zg = jax.nn.gelu(gmlp_in, approximate=True)
    u = zg[:, 0:dg]
    v = _layer_norm(zg[:, dg:2 * dg], glng_ref[...], glnb_ref[...]).astype(BF16)
    lane_g = lax.broadcasted_iota(jnp.int32, (GMLP_BLOCK, dg), 1)
    gw = gw_ref[...]
    gbias = gbias_ref[...]
    gates = []
    for nb in range(ts // GMLP_BLOCK):
        vb = v[nb * GMLP_BLOCK:(nb + 1) * GMLP_BLOCK, :]
        r = jnp.dot(gw, vb, preferred_element_type=F32)
        g = r[0:GMLP_BLOCK, :]
        for h in range(1, GMLP_HEADS):
            g = jnp.where(lane_g >= h * GMLP_HEAD_DIM, r[h * GMLP_BLOCK:(h + 1) * GMLP_BLOCK, :], g)
        gates.append(g + gbias)
    y_gmlp = u * jnp.concatenate(gates, axis=0)

    kv = kv_ref[0]
    mk = kv[:, 0:dg]
    mv = kv[:, dg:2 * dg]
    lane_q = lax.broadcasted_iota(jnp.int32, (ts, dg), 1)
    qs = mem_q * (MEM_HEAD_DIM ** -0.5)
    y_mem = jnp.zeros((ts, dg), F32)
    for h in range(MEM_HEADS):
        in_head = (lane_q >= h * MEM_HEAD_DIM) & (lane_q < (h + 1) * MEM_HEAD_DIM)
        qh = jnp.where(in_head, qs, 0.0).astype(BF16)
        sc = lax.dot_general(qh, mk, (((1,), (1,)), ((), ())), preferred_element_type=F32)
        e = jnp.exp(sc - jnp.max(sc, axis=-1, keepdims=True))
        den = jnp.sum(e, axis=-1, keepdims=True)
        oh = jnp.dot(e.astype(BF16), mv, preferred_element_type=F32) / den
        y_mem = jnp.where(in_head, oh, y_mem)

    gng = gng_ref[...]
    groups = []
    for gi, y in enumerate((y_conv, y_pool, y_gmlp, y_mem)):
        ms = jnp.mean(y * y, axis=-1, keepdims=True)
        groups.append(y * lax.rsqrt(ms + RMS_EPS) * gng[:, gi * dg:(gi + 1) * dg])
    cat = jnp.concatenate(groups, axis=1).astype(BF16)
    mix = jnp.dot(cat, wout_ref[...], preferred_element_type=F32)

    h1 = _layer_norm(DEEPNORM_ALPHA * x + mix, ln1g_ref[...], ln1b_ref[...])
    for j in range(ROW_TILE_ROWS):
        h1rt_ref[pl.ds(j, ts, stride=ROW_TILE_ROWS), :] = h1[:, j * LANES:(j + 1) * LANES]

    h1_hi = h1.astype(BF16)
    h1_lo = (h1 - h1_hi.astype(F32)).astype(BF16)
    logits = (jnp.dot(h1_hi, wrh_ref[...], preferred_element_type=F32)
              + jnp.dot(h1_lo, wrh_ref[...], preferred_element_type=F32)
              + jnp.dot(h1_hi, wrl_ref[...], preferred_element_type=F32)) + br_ref[...]
    lane = lax.broadcasted_iota(jnp.int32, (ts, LANES), 1)
    lane_f = lane.astype(F32)
    vals, idxs = [], []
    cur_l = logits
    for _ in range(TOP_K):
        m = jnp.max(cur_l, axis=-1, keepdims=True)
        i_f = jnp.min(jnp.where(cur_l == m, lane_f, float(LANES)), axis=-1, keepdims=True)
        vals.append(m)
        idxs.append(i_f)
        cur_l = jnp.where(lane_f == i_f, -jnp.inf, cur_l)
    es = [jnp.exp(vk - vals[0]) for vk in vals]
    den = es[0] + es[1] + es[2] + es[3]
    idx_out = jnp.zeros((ts, LANES), F32)
    gate_out = jnp.zeros((ts, LANES), F32)
    hits = jnp.zeros((ts, LANES), F32)
    for k in range(TOP_K):
        idx_out = jnp.where(lane == k, idxs[k], idx_out)
        gate_out = jnp.where(lane == k, es[k] / den, gate_out)
        hits = hits + jnp.where(lane_f == idxs[k], 1.0, 0.0)
    tok = (pl.program_id(0) * pl.num_programs(1) + s) * ts + lax.broadcasted_iota(jnp.int32, (ts, LANES), 0)
    group = (tok >> block_shift) * N_EXPERTS + idx_out.astype(jnp.int32)
    key_ref[...] = jnp.where(lane < TOP_K, group * (1 << key_shift) + tok * TOP_K + lane, 0)
    gate_ref[...] = gate_out
    cnt_ref[0] = jnp.sum(hits, axis=0, keepdims=True)


def _mixer(h, kv_all, layer, lw, *, bsz, seq, ts, key_shift, block_shift):
    n = bsz * seq
    st = seq // ts
    dg = D_GROUP
    const = lambda b, s: (0, 0)
    kern = functools.partial(_mixer_kernel, ts=ts, key_shift=key_shift, block_shift=block_shift)
    return pl.pallas_call(
        kern,
        grid=(bsz, st),
        in_specs=[
            pl.BlockSpec((ts * ROW_TILE_ROWS, LANES), lambda b, s: (b * st + s, 0)),
            pl.BlockSpec((1, kv_all.shape[1], 2 * dg), lambda b, s: (b, 0, layer)),
            pl.BlockSpec((D_MODEL, D_IN_PROJ), const),
            pl.BlockSpec((3, dg), const),
            pl.BlockSpec((dg, dg), const),
            pl.BlockSpec((1, dg), const),
            pl.BlockSpec((1, dg), const),
            pl.BlockSpec((1, dg), const),
            pl.BlockSpec((GMLP_HEADS * GMLP_BLOCK, GMLP_BLOCK), const),
            pl.BlockSpec((GMLP_BLOCK, dg), const),
            pl.BlockSpec((1, D_MODEL), const),
            pl.BlockSpec((D_MODEL, D_MODEL), const),
            pl.BlockSpec((1, D_MODEL), const),
            pl.BlockSpec((1, D_MODEL), const),
            pl.BlockSpec((D_MODEL, LANES), const),
            pl.BlockSpec((D_MODEL, LANES), const),
            pl.BlockSpec((1, LANES), const),
        ],
        out_specs=[
            pl.BlockSpec((ts * ROW_TILE_ROWS, LANES), lambda b, s: (b * st + s, 0)),
            pl.BlockSpec((ts, LANES), lambda b, s: (b * st + s, 0)),
            pl.BlockSpec((ts, LANES), lambda b, s: (b * st + s, 0)),
            pl.BlockSpec((1, 1, LANES), lambda b, s: (b * st + s, 0, 0)),
        ],
        out_shape=[
            jax.ShapeDtypeStruct((n * ROW_TILE_ROWS, LANES), F32),
            jax.ShapeDtypeStruct((n, LANES), jnp.int32),
            jax.ShapeDtypeStruct((n, LANES), F32),
            jax.ShapeDtypeStruct((bsz * st, 1, LANES), F32),
        ],
        scratch_shapes=[pltpu.VMEM((ts + HALO, 2 * dg), F32)],
        compiler_params=pltpu.CompilerParams(
            dimension_semantics=("arbitrary", "arbitrary"), vmem_limit_bytes=VMEM_LIMIT_BYTES),
        name="mixer",
    )(h, kv_all, *lw)


def _expert_kernel(te_ref, tb_ref, nact_ref, row_ref, row_next_ref, slot_prev_ref, slot_ref, gate_ref,
                   h_hbm, w1_ref, b1_ref, w2_ref, b2_ref, g_ref, bt_ref, o_hbm,
                   xblk, ybuf, sin, sout, hbuf, xsem, osem, *, tm, tbt):
    i = pl.program_id(0)
    nt = pl.num_programs(0)
    n_act = nact_ref[0]
    par = lax.rem(i, 2)
    rt = ROW_TILE_ROWS
    blk_rows = tbt * rt
    sp = STAGE_PITCH
    cw = D_FF // EXPERT_CHUNKS

    def zero_row(tiles, width):
        bits = pltpu.bitcast(tiles[0], jnp.uint32)
        for t in tiles[1:]:
            bits = bits | pltpu.bitcast(t, jnp.uint32)
        z = ((bits >> 16) >> 16).astype(F32)[0:1, :]
        return jnp.concatenate([z] * (width // LANES), axis=1)

    def gather(rows_ref, dst, lo=0, hi=tm):
        tiles = []
        for r in range(lo, hi):
            tiles.append(xblk[pl.ds(pl.multiple_of(rows_ref[0, 0, r], rt), rt), :])
            dst[pl.ds(r * rt, rt), :] = tiles[-1]
        return tiles

    def accumulate(slots_ref, src, lo=0, hi=tm):
        sums = []
        for r0 in range(lo, hi, ROW_BATCH):
            rs = range(r0, r0 + ROW_BATCH)
            slots = [slots_ref[0, 0, r] for r in rs]
            offs = [pl.multiple_of((s >> 2) * rt, rt) for s in slots]
            accs = [ybuf[pl.ds(o, rt), :] for o in offs]
            vals = [src[pl.ds(r, rt, stride=sp), :] for r in rs]
            for k, r in enumerate(rs):
                gate = gate_ref[0, 0, slots[k]]
                sums.append(accs[k] + gate * vals[k])
                ybuf[pl.ds(offs[k], rt), :] = sums[-1]
        return sums

    @pl.when(i == 0)
    def _():
        xblk[pl.ds(blk_rows, rt), :] = jnp.zeros((rt, LANES), F32)
        sout[...] = jnp.zeros(sout.shape, F32)

    @pl.when(i < n_act)
    def _():
        b = tb_ref[i]
        first = jnp.logical_or(i == 0, tb_ref[jnp.maximum(i - 1, 0)] != b)
        last = jnp.logical_or(i == n_act - 1, tb_ref[jnp.minimum(i + 1, nt - 1)] != b)
        row0 = pl.multiple_of(b * blk_rows, blk_rows)

        @pl.when(first)
        def _():
            load = pltpu.make_async_copy(h_hbm.at[pl.ds(row0, blk_rows), :], xblk.at[pl.ds(0, blk_rows), :], xsem)
            load.start()
            zrows = FINAL_CHUNK * rt

            def zero(c, carry):
                ybuf[pl.ds(pl.multiple_of(c * zrows, zrows), zrows), :] = jnp.zeros((zrows, LANES), F32)
                return carry

            lax.fori_loop(0, tbt // FINAL_CHUNK, zero, 0)
            ybuf[pl.ds(blk_rows, rt), :] = jnp.zeros((rt, LANES), F32)
            load.wait()
            gather(row_ref, sin.at[par])

        xs = sin.at[par]
        x = jnp.concatenate([xs[pl.ds(j, tm, stride=rt), :] for j in range(rt)], axis=1)
        xb = x.astype(BF16)

        per1 = tm // (2 * EXPERT_CHUNKS)
        per2 = tm // EXPERT_CHUNKS
        for c in range(2 * EXPERT_CHUNKS):
            tie = zero_row(accumulate(slot_prev_ref, sout.at[1 - par], c * per1, (c + 1) * per1), cw)
            hbuf[:, c * cw:(c + 1) * cw] = (jnp.dot(xb, w1_ref[:, c * cw:(c + 1) * cw], preferred_element_type=F32)
                                            + (b1_ref[:, c * cw:(c + 1) * cw] + tie))
        out = None
        for c in range(EXPERT_CHUNKS):
            tie = zero_row(gather(row_next_ref, sin.at[1 - par], c * per2, (c + 1) * per2), cw)
            glu = jnp.minimum(hbuf[:, c * cw:(c + 1) * cw], SWIGLU_LIMIT)
            lin = jnp.clip(hbuf[:, D_FF + c * cw:D_FF + (c + 1) * cw] + tie, -SWIGLU_LIMIT, SWIGLU_LIMIT)
            act = glu * jax.nn.sigmoid(SWIGLU_ALPHA * glu) * (lin + 1.0)
            part = jnp.dot(act.astype(BF16), w2_ref[c * cw:(c + 1) * cw, :], preferred_element_type=F32)
            out = part if out is None else out + part
        out = out + b2_ref[...]
        so = sout.at[par]
        for j in range(rt):
            so[pl.ds(j * sp, tm), :] = out[:, j * LANES:(j + 1) * LANES]

        @pl.when(last)
        def _():
            accumulate(slot_ref, so)
            so[...] = jnp.zeros(so.shape, F32)
            crows = FINAL_CHUNK * rt
            g3 = g_ref[...].reshape(1, rt, LANES)
            b3 = bt_ref[...].reshape(1, rt, LANES)

            def finish(c, carry):
                rows_c = pl.ds(pl.multiple_of(c * crows, crows), crows)
                r3 = (DEEPNORM_ALPHA * xblk[rows_c, :] + ybuf[rows_c, :]).reshape(FINAL_CHUNK, rt, LANES)
                mu = jnp.sum(jnp.sum(r3, axis=2, keepdims=True), axis=1, keepdims=True) * (1.0 / D_MODEL)
                xc = r3 - mu
                var = jnp.sum(jnp.sum(xc * xc, axis=2, keepdims=True), axis=1, keepdims=True) * (1.0 / D_MODEL)
                y3 = xc * lax.rsqrt(var + LN_EPS) * g3 + b3
                ybuf[rows_c, :] = y3.reshape(crows, LANES)
                pltpu.make_async_copy(
                    ybuf.at[rows_c, :],
                    o_hbm.at[pl.ds(pl.multiple_of(row0 + c * crows, crows), crows), :], osem).start()
                return carry

            lax.fori_loop(0, tbt // FINAL_CHUNK, finish, 0)
            pltpu.make_async_copy(ybuf.at[pl.ds(0, blk_rows), :], o_hbm.at[pl.ds(row0, blk_rows), :], osem).wait()


def _experts(h1rt, te, tb, nact, rowoff, slot, gates_blk, w1b, b1, w2b, b2, ln_g, ln_b, layer, *, tm, tbt):
    nt = rowoff.shape[0]
    n_rows = h1rt.shape[0]
    kern = functools.partial(_expert_kernel, tm=tm, tbt=tbt)
    smem_blk = lambda d: pl.BlockSpec((1, 1, tm), lambda i, te, tb, na: (jnp.clip(i + d, 0, nt - 1), 0, 0),
                                      memory_space=pltpu.SMEM)
    per_expert = lambda i, te, tb, na: (layer, te[i], 0, 0)
    const = lambda i, te, tb, na: (0, 0)
    grid_spec = pltpu.PrefetchScalarGridSpec(
        num_scalar_prefetch=3,
        grid=(nt,),
        in_specs=[
            smem_blk(0), smem_blk(1),
            smem_blk(-1), smem_blk(0),
            pl.BlockSpec((1, 1, gates_blk.shape[2]), lambda i, te, tb, na: (tb[i], 0, 0), memory_space=pltpu.SMEM),
            pl.BlockSpec(memory_space=pl.ANY),
            pl.BlockSpec((None, None, D_MODEL, 2 * D_FF), per_expert),
            pl.BlockSpec((None, None, 1, 2 * D_FF), per_expert),
            pl.BlockSpec((None, None, D_FF, D_MODEL), per_expert),
            pl.BlockSpec((None, None, 1, D_MODEL), per_expert),
            pl.BlockSpec((ROW_TILE_ROWS, LANES), const),
            pl.BlockSpec((ROW_TILE_ROWS, LANES), const),
        ],
        out_specs=pl.BlockSpec(memory_space=pl.ANY),
        scratch_shapes=[
            pltpu.VMEM(((tbt + 1) * ROW_TILE_ROWS, LANES), F32),
            pltpu.VMEM(((tbt + 1) * ROW_TILE_ROWS, LANES), F32),
            pltpu.VMEM((2, tm * ROW_TILE_ROWS, LANES), F32),
            pltpu.VMEM((2, ROW_TILE_ROWS * STAGE_PITCH, LANES), F32),
            pltpu.VMEM((tm, 2 * D_FF), F32),
            pltpu.SemaphoreType.DMA,
            pltpu.SemaphoreType.DMA,
        ],
    )
    return pl.pallas_call(
        kern,
        grid_spec=grid_spec,
        out_shape=jax.ShapeDtypeStruct((n_rows, LANES), F32),
        compiler_params=pltpu.CompilerParams(
            dimension_semantics=("arbitrary",), vmem_limit_bytes=VMEM_LIMIT_BYTES),
        name="experts",
    )(te, tb, nact, rowoff, rowoff, slot, slot, gates_blk, h1rt, w1b, b1, w2b, b2, ln_g, ln_b)


def _key_shift(n, n_groups, tm):
    return max((n * TOP_K - 1).bit_length(), (n_groups * tm - 1).bit_length()) + 1


def _routing_plan(keys, counts, n, tm, tbt):
    nb = n // tbt
    n_groups = nb * N_EXPERTS
    na = n * TOP_K
    n_tiles = na // tm + n_groups
    shift = _key_shift(n, n_groups, tm)
    flag = 1 << (shift - 1)
    need = ((-counts) % tm).reshape(n_groups, 1)
    g_ids = jnp.arange(n_groups, dtype=jnp.int32)[:, None]
    j_ids = jnp.arange(tm, dtype=jnp.int32)[None, :]
    pad_keys = jnp.where(j_ids < need, g_ids * (1 << shift) + flag + j_ids,
                         n_groups * (1 << shift) + flag + g_ids * tm + j_ids)
    ks = jnp.sort(jnp.concatenate([keys, pad_keys.reshape(-1)]))
    low = ks & (flag - 1)
    real = (ks & flag) == 0
    slot = jnp.where(real, low & (tbt * TOP_K - 1), tbt * TOP_K)
    rowoff = (slot >> 2) * ROW_TILE_ROWS
    g_t = ks[::tm] >> shift
    active = g_t < n_groups
    te = jnp.where(active, g_t & (N_EXPERTS - 1), N_EXPERTS - 1)
    tb = jnp.minimum(g_t >> 5, nb - 1)
    nact = jnp.sum(active.astype(jnp.int32)).reshape(1)
    return te, tb, nact, rowoff.reshape(n_tiles, 1, tm), slot.reshape(n_tiles, 1, tm)


def _pick_tile(total, want):
    t = min(total, want)
    while total % t:
        t //= 2
    return t


def kernel(x, mem, ln_in_g, ln_in_b, mem_ln_g, mem_ln_b, w_in, conv_w, pool_w, pool_scale, gmlp_ln_g, gmlp_ln_b,
           gmlp_w, gmlp_b, w_mem_kv, group_norm_g, w_out, ln1_g, ln1_b, w_router, b_router, w1, b1, w2, b2,
           ln2_g, ln2_b):
    bsz, seq, d = x.shape
    depth = w_in.shape[0]
    n = bsz * seq
    dg = D_GROUP
    ts = _pick_tile(seq, 512)
    tm = EXPERT_TILE
    tbt = _pick_tile(n, EXPERT_BLOCK_TOKENS)
    nb = n // tbt
    assert N_EXPERTS == 32 and tbt % ts == 0 and tbt & (tbt - 1) == 0 and tbt % FINAL_CHUNK == 0
    key_shift = _key_shift(n, nb * N_EXPERTS, tm)
    block_shift = tbt.bit_length() - 1

    h = _ln_rows(x.reshape(n, d), ln_in_g, ln_in_b, _pick_tile(n, 1024))

    wkv_all = jnp.transpose(w_mem_kv, (1, 0, 2)).reshape(d, depth * 2 * dg).astype(BF16)
    kv_all = _mem_kv(mem, mem_ln_g, mem_ln_b, wkv_all)

    pos = jnp.arange(GMLP_BLOCK)
    mask = (pos[None, :] // CHUNK) <= (pos[:, None] // CHUNK)
    eye = jnp.eye(len(POOL_WINDOWS), dtype=F32)

    wr_pad = jnp.zeros((depth, d, LANES), F32).at[:, :, :N_EXPERTS].set(w_router)
    wr_hi = wr_pad.astype(BF16)
    wr_lo = (wr_pad - wr_hi.astype(F32)).astype(BF16)
    br_pad = jnp.full((depth, 1, LANES), NEG_BIG, F32).at[:, 0, :N_EXPERTS].set(b_router)
    w1b = w1.astype(BF16)
    w2b = w2.astype(BF16)
    b1r = b1.reshape(depth, N_EXPERTS, 1, 2 * D_FF)
    b2r = b2.reshape(depth, N_EXPERTS, 1, d)

    for l in range(depth):
        pool_bd = jnp.einsum('gh,gcd->gchd', eye, pool_w[l]).reshape(dg, dg).astype(BF16)
        gw = jnp.where(mask[None], gmlp_w[l], 0.0).reshape(GMLP_HEADS * GMLP_BLOCK, GMLP_BLOCK).astype(BF16)
        gbias = jnp.repeat(gmlp_b[l].T, GMLP_HEAD_DIM, axis=1)
        lw = (w_in[l].astype(BF16), conv_w[l], pool_bd, pool_scale[l].reshape(1, dg),
              gmlp_ln_g[l].reshape(1, dg), gmlp_ln_b[l].reshape(1, dg), gw, gbias,
              group_norm_g[l].reshape(1, d), w_out[l].astype(BF16),
              ln1_g[l].reshape(1, d), ln1_b[l].reshape(1, d), wr_hi[l], wr_lo[l], br_pad[l])
        h1rt, keys, gates, cnt = _mixer(h, kv_all, l, lw, bsz=bsz, seq=seq, ts=ts,
                                        key_shift=key_shift, block_shift=block_shift)
        counts = jnp.sum(cnt.reshape(nb, tbt // ts, LANES), axis=1)[:, :N_EXPERTS].astype(jnp.int32)
        te, tb, nact, rowoff, slot = _routing_plan(keys[:, :TOP_K].reshape(-1), counts, n, tm, tbt)
        gates_blk = jnp.pad(gates[:, :TOP_K].reshape(nb, 1, tbt * TOP_K), ((0, 0), (0, 0), (0, LANES)))
        h = _experts(h1rt, te, tb, nact, rowoff, slot, gates_blk, w1b, b1r, w2b, b2r,
                     ln2_g[l].reshape(ROW_TILE_ROWS, LANES), ln2_b[l].reshape(ROW_TILE_ROWS, LANES), l,
                     tm=tm, tbt=tbt)
    return h.reshape(bsz, seq, d)
```

```python
import functools

import jax
import jax.numpy as jnp
from jax import lax
from jax.experimental import pallas as pl
from jax.experimental.pallas import tpu as pltpu

F32 = jnp.float32
BF16 = jnp.bfloat16

D_MODEL = 1024
N_GROUPS = 4
D_GROUP = D_MODEL // N_GROUPS
D_IN_PROJ = 7 * D_GROUP
POOL_WINDOWS = (2, 4, 8, 16)
POOL_CH = D_GROUP // len(POOL_WINDOWS)
GMLP_BLOCK = 128
GMLP_HEADS = 4
GMLP_HEAD_DIM = D_GROUP // GMLP_HEADS
CHUNK = 64
MEM_HEADS = 4
MEM_HEAD_DIM = D_GROUP // MEM_HEADS
N_EXPERTS = 32
GROUP_SLOTS = N_EXPERTS + 1
TOP_K = 4
D_FF = D_MODEL
SWIGLU_ALPHA = 1.702
SWIGLU_LIMIT = 7.0
REF_DEPTH = 4
DEEPNORM_ALPHA = (2 * REF_DEPTH) ** 0.25
LN_EPS = 1e-5
RMS_EPS = 1e-6

SUBLANES = 8
LANES = 128
ROW_TILE_ROWS = D_MODEL // LANES
HALO = max(POOL_WINDOWS)
VMEM_LIMIT_BYTES = 58 * 1024 * 1024
EXPERT_CHUNKS = 4
EXPERT_TILE = 256
EXPERT_BLOCK_TOKENS = 4096
STAGE_PITCH = EXPERT_TILE + SUBLANES
FINAL_CHUNK = 128
ROW_BATCH = 8

NEG_BIG = -1e30


def _layer_norm(x, g, b):
    mu = jnp.mean(x, axis=-1, keepdims=True)
    xc = x - mu
    var = jnp.mean(xc * xc, axis=-1, keepdims=True)
    return xc * lax.rsqrt(var + LN_EPS) * g + b


def _ln_kernel(x_ref, g_ref, b_ref, o_ref, *, rows):
    y = _layer_norm(x_ref[...], g_ref[...], b_ref[...])
    for j in range(ROW_TILE_ROWS):
        o_ref[pl.ds(j, rows, stride=ROW_TILE_ROWS), :] = y[:, j * LANES:(j + 1) * LANES]


def _ln_rows(x, g, b, rows):
    n, d = x.shape
    return pl.pallas_call(
        functools.partial(_ln_kernel, rows=rows),
        grid=(n // rows,),
        in_specs=[pl.BlockSpec((rows, d), lambda i: (i, 0)),
                  pl.BlockSpec((1, d), lambda i: (0, 0)),
                  pl.BlockSpec((1, d), lambda i: (0, 0))],
        out_specs=pl.BlockSpec((rows * ROW_TILE_ROWS, LANES), lambda i: (i, 0)),
        out_shape=jax.ShapeDtypeStruct((n * ROW_TILE_ROWS, LANES), F32),
        name="entry_ln",
    )(x, g.reshape(1, d), b.reshape(1, d))


def _mem_kv_kernel(m_ref, g_ref, b_ref, w_ref, o_ref):
    mn = _layer_norm(m_ref[0], g_ref[...], b_ref[...])
    o_ref[0] = jnp.dot(mn.astype(BF16), w_ref[...], preferred_element_type=F32).astype(BF16)


def _mem_kv(mem, g, b, w_all):
    bsz, m, d = mem.shape
    p = w_all.shape[1]
    return pl.pallas_call(
        _mem_kv_kernel,
        grid=(bsz,),
        in_specs=[pl.BlockSpec((1, m, d), lambda i: (i, 0, 0)),
                  pl.BlockSpec((1, d), lambda i: (0, 0)),
                  pl.BlockSpec((1, d), lambda i: (0, 0)),
                  pl.BlockSpec((d, p), lambda i: (0, 0))],
        out_specs=pl.BlockSpec((1, m, p), lambda i: (i, 0, 0)),
        out_shape=jax.ShapeDtypeStruct((bsz, m, p), BF16),
        name="mem_kv",
    )(mem, g.reshape(1, d), b.reshape(1, d), w_all)


def _mixer_kernel(x_ref, kv_ref, win_ref, convw_ref, poolw_ref, pscale_ref, glng_ref, glnb_ref,
                  gw_ref, gbias_ref, gng_ref, wout_ref, ln1g_ref, ln1b_ref, wrh_ref, wrl_ref, br_ref,
                  h1rt_ref, key_ref, gate_ref, cnt_ref, ext_ref, *, ts, key_shift, block_shift):
    s = pl.program_id(1)
    dg = D_GROUP

    x = jnp.concatenate([x_ref[pl.ds(j, ts, stride=ROW_TILE_ROWS), :] for j in range(ROW_TILE_ROWS)], axis=1)
    proj = jnp.dot(x.astype(BF16), win_ref[...], preferred_element_type=F32)
    gate_b = proj[:, 0:dg]
    gate_c = proj[:, dg:2 * dg]
    conv_in = proj[:, 2 * dg:3 * dg]
    pool_in = proj[:, 3 * dg:4 * dg]
    gmlp_in = proj[:, 4 * dg:6 * dg]
    mem_q = proj[:, 6 * dg:7 * dg]

    @pl.when(s == 0)
    def _():
        ext_ref[0:HALO, :] = jnp.zeros((HALO, 2 * dg), F32)

    z = gate_c * conv_in
    ext_ref[HALO:HALO + ts, 0:dg] = z
    ext_ref[HALO:HALO + ts, dg:2 * dg] = pool_in

    zm1 = ext_ref[pl.ds(HALO - 1, ts), 0:dg]
    zm2 = ext_ref[pl.ds(HALO - 2, ts), 0:dg]
    cw = convw_ref[...]
    y_conv = gate_b * (cw[0:1, :] * zm2 + cw[1:2, :] * zm1 + cw[2:3, :] * z)

    pos = s * ts + lax.broadcasted_iota(jnp.int32, (ts, 1), 0)
    half = dg // 2
    lane_h = lax.broadcasted_iota(jnp.int32, (ts, half), 1)
    diffs = []
    for hf in range(2):
        c0 = dg + hf * half
        w_lo, w_hi = POOL_WINDOWS[2 * hf], POOL_WINDOWS[2 * hf + 1]
        cur = ext_ref[pl.ds(HALO, ts), c0:c0 + half]
        acc = cur
        for k in range(1, w_lo):
            acc = acc + ext_ref[pl.ds(HALO - k, ts), c0:c0 + half]
        s_lo = acc
        for k in range(w_lo, w_hi):
            acc = acc + ext_ref[pl.ds(HALO - k, ts), c0:c0 + half]
        s_hi = acc
        cnt_lo = jnp.minimum(pos + 1, w_lo).astype(F32)
        cnt_hi = jnp.minimum(pos + 1, w_hi).astype(F32)
        mean = jnp.where(lane_h < POOL_CH, s_lo / cnt_lo, s_hi / cnt_hi)
        diffs.append(mean - cur)
    dpool = jnp.concatenate(diffs, axis=1)
    y_pool = jnp.dot(dpool.astype(BF16), poolw_ref[...], preferred_element_type=F32) * pscale_ref[...]

    ext_ref[0:HALO, :] = ext_ref[ts:ts + HALO, :]

    zg = jax.nn.gelu(gmlp_in, approximate=True)
    u = zg[:, 0:dg]
    v = _layer_norm(zg[:, dg:2 * dg], glng_ref[...], glnb_ref[...]).astype(BF16)
    lane_g = lax.broadcasted_iota(jnp.int32, (GMLP_BLOCK, dg), 1)
    gw = gw_ref[...]
    gbias = gbias_ref[...]
    gates = []
    for nb in range(ts // GMLP_BLOCK):
        vb = v[nb * GMLP_BLOCK:(nb + 1) * GMLP_BLOCK, :]
        r = jnp.dot(gw, vb, preferred_element_type=F32)
        g = r[0:GMLP_BLOCK, :]
        for h in range(1, GMLP_HEADS):
            g = jnp.where(lane_g >= h * GMLP_HEAD_DIM, r[h * GMLP_BLOCK:(h + 1) * GMLP_BLOCK, :], g)
        gates.append(g + gbias)
    y_gmlp = u * jnp.concatenate(gates, axis=0)

    kv = kv_ref[0]
    mk = kv[:, 0:dg]
    mv = kv[:, dg:2 * dg]
    lane_q = lax.broadcasted_iota(jnp.int32, (ts, dg), 1)
    qs = mem_q * (MEM_HEAD_DIM ** -0.5)
    y_mem = jnp.zeros((ts, dg), F32)
    for h in range(MEM_HEADS):
        in_head = (lane_q >= h * MEM_HEAD_DIM) & (lane_q < (h + 1) * MEM_HEAD_DIM)
        qh = jnp.where(in_head, qs, 0.0).astype(BF16)
        sc = lax.dot_general(qh, mk, (((1,), (1,)), ((), ())), preferred_element_type=F32)
        e = jnp.exp(sc - jnp.max(sc, axis=-1, keepdims=True))
        den = jnp.sum(e, axis=-1, keepdims=True)
        oh = jnp.dot(e.astype(BF16), mv, preferred_element_type=F32) / den
        y_mem = jnp.where(in_head, oh, y_mem)

    gng = gng_ref[...]
    groups = []
    for gi, y in enumerate((y_conv, y_pool, y_gmlp, y_mem)):
        ms = jnp.mean(y * y, axis=-1, keepdims=True)
        groups.append(y * lax.rsqrt(ms + RMS_EPS) * gng[:, gi * dg:(gi + 1) * dg])
    cat = jnp.concatenate(groups, axis=1).astype(BF16)
    mix = jnp.dot(cat, wout_ref[...], preferred_element_type=F32)

    h1 = _layer_norm(DEEPNORM_ALPHA * x + mix, ln1g_ref[...], ln1b_ref[...])
    for j in range(ROW_TILE_ROWS):
        h1rt_ref[pl.ds(j, ts, stride=ROW_TILE_ROWS), :] = h1[:, j * LANES:(j + 1) * LANES]

    h1_hi = h1.astype(BF16)
    h1_lo = (h1 - h1_hi.astype(F32)).astype(BF16)
    logits = (jnp.dot(h1_hi, wrh_ref[...], preferred_element_type=F32)
              + jnp.dot(h1_lo, wrh_ref[...], preferred_element_type=F32)
              + jnp.dot(h1_hi, wrl_ref[...], preferred_element_type=F32)) + br_ref[...]
    lane = lax.broadcasted_iota(jnp.int32, (ts, LANES), 1)
    lane_f = lane.astype(F32)
    vals, idxs = [], []
    cur_l = logits
    for _ in range(TOP_K):
        m = jnp.max(cur_l, axis=-1, keepdims=True)
        i_f = jnp.min(jnp.where(cur_l == m, lane_f, float(LANES)), axis=-1, keepdims=True)
        vals.append(m)
        idxs.append(i_f)
        cur_l = jnp.where(lane_f == i_f, -jnp.inf, cur_l)
    es = [jnp.exp(vk - vals[0]) for vk in vals]
    den = es[0] + es[1] + es[2] + es[3]
    idx_out = jnp.zeros((ts, LANES), F32)
    gate_out = jnp.zeros((ts, LANES), F32)
    hits = jnp.zeros((ts, LANES), F32)
    for k in range(TOP_K):
        idx_out = jnp.where(lane == k, idxs[k], idx_out)
        gate_out = jnp.where(lane == k, es[k] / den, gate_out)
        hits = hits + jnp.where(lane_f == idxs[k], 1.0, 0.0)
    tok = (pl.program_id(0) * pl.num_programs(1) + s) * ts + lax.broadcasted_iota(jnp.int32, (ts, LANES), 0)
    group = (tok >> block_shift) * GROUP_SLOTS + idx_out.astype(jnp.int32)
    key_ref[...] = jnp.where(lane < TOP_K, group * (1 << key_shift) + tok * TOP_K + lane, 0)
    gate_ref[...] = gate_out
    cnt_ref[0] = jnp.sum(hits, axis=0, keepdims=True)


def _mixer(h, kv_all, layer, lw, *, bsz, seq, ts, key_shift, block_shift):
    n = bsz * seq
    st = seq // ts
    dg = D_GROUP
    const = lambda b, s: (0, 0)
    kern = functools.partial(_mixer_kernel, ts=ts, key_shift=key_shift, block_shift=block_shift)
    return pl.pallas_call(
        kern,
        grid=(bsz, st),
        in_specs=[
            pl.BlockSpec((ts * ROW_TILE_ROWS, LANES), lambda b, s: (b * st + s, 0)),
            pl.BlockSpec((1, kv_all.shape[1], 2 * dg), lambda b, s: (b, 0, layer)),
            pl.BlockSpec((D_MODEL, D_IN_PROJ), const),
            pl.BlockSpec((3, dg), const),
            pl.BlockSpec((dg, dg), const),
            pl.BlockSpec((1, dg), const),
            pl.BlockSpec((1, dg), const),
            pl.BlockSpec((1, dg), const),
            pl.BlockSpec((GMLP_HEADS * GMLP_BLOCK, GMLP_BLOCK), const),
            pl.BlockSpec((GMLP_BLOCK, dg), const),
            pl.BlockSpec((1, D_MODEL), const),
            pl.BlockSpec((D_MODEL, D_MODEL), const),
            pl.BlockSpec((1, D_MODEL), const),
            pl.BlockSpec((1, D_MODEL), const),
            pl.BlockSpec((D_MODEL, LANES), const),
            pl.BlockSpec((D_MODEL, LANES), const),
            pl.BlockSpec((1, LANES), const),
        ],
        out_specs=[
            pl.BlockSpec((ts * ROW_TILE_ROWS, LANES), lambda b, s: (b * st + s, 0)),
            pl.BlockSpec((ts, LANES), lambda b, s: (b * st + s, 0)),
            pl.BlockSpec((ts, LANES), lambda b, s: (b * st + s, 0)),
            pl.BlockSpec((1, 1, LANES), lambda b, s: (b * st + s, 0, 0)),
        ],
        out_shape=[
            jax.ShapeDtypeStruct((n * ROW_TILE_ROWS, LANES), F32),
            jax.ShapeDtypeStruct((n, LANES), jnp.int32),
            jax.ShapeDtypeStruct((n, LANES), F32),
            jax.ShapeDtypeStruct((bsz * st, 1, LANES), F32),
        ],
        scratch_shapes=[pltpu.VMEM((ts + HALO, 2 * dg), F32)],
        compiler_params=pltpu.CompilerParams(
            dimension_semantics=("arbitrary", "arbitrary"), vmem_limit_bytes=VMEM_LIMIT_BYTES),
        name="mixer",
    )(h, kv_all, *lw)


def _expert_kernel(te_ref, nact_ref, slot_ref, gate_ref,
                   h_hbm, w1_ref, b1_ref, w2_ref, b2_ref, g_ref, bt_ref, o_hbm,
                   xblk, ybuf, sin, sout, hbuf, xsem, osem, *, tm, tbt, tpb):
    i = pl.program_id(0)
    b = i // tpb
    tj = i - b * tpb
    n_act = nact_ref[b]
    par = lax.rem(i, 2)
    rt = ROW_TILE_ROWS
    blk_rows = tbt * rt
    sp = STAGE_PITCH
    cw = D_FF // EXPERT_CHUNKS
    base = (tj + 1) * tm

    def zero_row(tiles, width):
        bits = pltpu.bitcast(tiles[0], jnp.uint32)
        for t in tiles[1:]:
            bits = bits | pltpu.bitcast(t, jnp.uint32)
        z = ((bits >> 16) >> 16).astype(F32)[0:1, :]
        return jnp.concatenate([z] * (width // LANES), axis=1)

    def gather(first_entry, dst, lo=0, hi=tm):
        tiles = []
        entries = slot_ref.at[0, 0, pl.ds(first_entry, tm)]
        for r in range(lo, hi):
            off = pl.multiple_of((entries[r] >> 2) * rt, rt)
            tiles.append(xblk[pl.ds(off, rt), :])
            dst[pl.ds(r * rt, rt), :] = tiles[-1]
        return tiles

    def accumulate(first_entry, src, lo=0, hi=tm):
        sums = []
        entries = slot_ref.at[0, 0, pl.ds(first_entry, tm)]
        for r0 in range(lo, hi, ROW_BATCH):
            rs = range(r0, r0 + ROW_BATCH)
            slots = [entries[r] for r in rs]
            offs = [pl.multiple_of((s >> 2) * rt, rt) for s in slots]
            accs = [ybuf[pl.ds(o, rt), :] for o in offs]
            vals = [src[pl.ds(r, rt, stride=sp), :] for r in rs]
            for k, r in enumerate(rs):
                gate = gate_ref[0, 0, slots[k]]
                sums.append(accs[k] + gate * vals[k])
                ybuf[pl.ds(offs[k], rt), :] = sums[-1]
        return sums

    @pl.when(i == 0)
    def _():
        xblk[pl.ds(blk_rows, rt), :] = jnp.zeros((rt, LANES), F32)
        sout[...] = jnp.zeros(sout.shape, F32)

    @pl.when(tj < n_act)
    def _():
        first = tj == 0
        last = tj == n_act - 1
        row0 = pl.multiple_of(b * blk_rows, blk_rows)

        @pl.when(first)
        def _():
            load = pltpu.make_async_copy(h_hbm.at[pl.ds(row0, blk_rows), :], xblk.at[pl.ds(0, blk_rows), :], xsem)
            load.start()
            zrows = FINAL_CHUNK * rt

            def zero(c, carry):
                ybuf[pl.ds(pl.multiple_of(c * zrows, zrows), zrows), :] = jnp.zeros((zrows, LANES), F32)
                return carry

            lax.fori_loop(0, tbt // FINAL_CHUNK, zero, 0)
            ybuf[pl.ds(blk_rows, rt), :] = jnp.zeros((rt, LANES), F32)
            load.wait()
            gather(base, sin.at[par])

        xs = sin.at[par]
        x = jnp.concatenate([xs[pl.ds(j, tm, stride=rt), :] for j in range(rt)], axis=1)
        xb = x.astype(BF16)

        per1 = tm // (2 * EXPERT_CHUNKS)
        per2 = tm // EXPERT_CHUNKS
        for c in range(2 * EXPERT_CHUNKS):
            tie = zero_row(accumulate(base - tm, sout.at[1 - par], c * per1, (c + 1) * per1), cw)
            hbuf[:, c * cw:(c + 1) * cw] = (jnp.dot(xb, w1_ref[:, c * cw:(c + 1) * cw], preferred_element_type=F32)
                                            + (b1_ref[:, c * cw:(c + 1) * cw] + tie))
        out = None
        for c in range(EXPERT_CHUNKS):
            tie = zero_row(gather(base + tm, sin.at[1 - par], c * per2, (c + 1) * per2), cw)
            glu = jnp.minimum(hbuf[:, c * cw:(c + 1) * cw], SWIGLU_LIMIT)
            lin = jnp.clip(hbuf[:, D_FF + c * cw:D_FF + (c + 1) * cw] + tie, -SWIGLU_LIMIT, SWIGLU_LIMIT)
            act = glu * jax.nn.sigmoid(SWIGLU_ALPHA * glu) * (lin + 1.0)
            part = jnp.dot(act.astype(BF16), w2_ref[c * cw:(c + 1) * cw, :], preferred_element_type=F32)
            out = part if out is None else out + part
        out = out + b2_ref[...]
        so = sout.at[par]
        for j in range(rt):
            so[pl.ds(j * sp, tm), :] = out[:, j * LANES:(j + 1) * LANES]

        @pl.when(last)
        def _():
            accumulate(base, so)
            crows = FINAL_CHUNK * rt
            g3 = g_ref[...].reshape(1, rt, LANES)
            b3 = bt_ref[...].reshape(1, rt, LANES)

            def finish(c, carry):
                rows_c = pl.ds(pl.multiple_of(c * crows, crows), crows)
                r3 = (DEEPNORM_ALPHA * xblk[rows_c, :] + ybuf[rows_c, :]).reshape(FINAL_CHUNK, rt, LANES)
                mu = jnp.sum(jnp.sum(r3, axis=2, keepdims=True), axis=1, keepdims=True) * (1.0 / D_MODEL)
                xc = r3 - mu
                var = jnp.sum(jnp.sum(xc * xc, axis=2, keepdims=True), axis=1, keepdims=True) * (1.0 / D_MODEL)
                y3 = xc * lax.rsqrt(var + LN_EPS) * g3 + b3
                ybuf[rows_c, :] = y3.reshape(crows, LANES)
                pltpu.make_async_copy(
                    ybuf.at[rows_c, :],
                    o_hbm.at[pl.ds(pl.multiple_of(row0 + c * crows, crows), crows), :], osem).start()
                return carry

            lax.fori_loop(0, tbt // FINAL_CHUNK, finish, 0)
            pltpu.make_async_copy(ybuf.at[pl.ds(0, blk_rows), :], o_hbm.at[pl.ds(row0, blk_rows), :], osem).wait()


def _experts(h1rt, te, nact, slots_blk, gates_blk, w1b, b1, w2b, b2, ln_g, ln_b, layer, *, tm, tbt):
    nt = te.shape[0]
    tpb = nt // nact.shape[0]
    n_rows = h1rt.shape[0]
    kern = functools.partial(_expert_kernel, tm=tm, tbt=tbt, tpb=tpb)
    per_block = lambda i, te, na: (i // tpb, 0, 0)
    per_expert = lambda i, te, na: (layer, te[i], 0, 0)
    const = lambda i, te, na: (0, 0)
    grid_spec = pltpu.PrefetchScalarGridSpec(
        num_scalar_prefetch=2,
        grid=(nt,),
        in_specs=[
            pl.BlockSpec((1, 1, slots_blk.shape[2]), per_block, memory_space=pltpu.SMEM,
                         pipeline_mode=pl.Buffered(1)),
            pl.BlockSpec((1, 1, gates_blk.shape[2]), per_block, memory_space=pltpu.SMEM,
                         pipeline_mode=pl.Buffered(1)),
            pl.BlockSpec(memory_space=pl.ANY),
            pl.BlockSpec((None, None, D_MODEL, 2 * D_FF), per_expert),
            pl.BlockSpec((None, None, 1, 2 * D_FF), per_expert),
            pl.BlockSpec((None, None, D_FF, D_MODEL), per_expert),
            pl.BlockSpec((None, None, 1, D_MODEL), per_expert),
            pl.BlockSpec((ROW_TILE_ROWS, LANES), const),
            pl.BlockSpec((ROW_TILE_ROWS, LANES), const),
        ],
        out_specs=pl.BlockSpec(memory_space=pl.ANY),
        scratch_shapes=[
            pltpu.VMEM(((tbt + 1) * ROW_TILE_ROWS, LANES), F32),
            pltpu.VMEM(((tbt + 1) * ROW_TILE_ROWS, LANES), F32),
            pltpu.VMEM((2, tm * ROW_TILE_ROWS, LANES), F32),
            pltpu.VMEM((2, ROW_TILE_ROWS * STAGE_PITCH, LANES), F32),
            pltpu.VMEM((tm, 2 * D_FF), F32),
            pltpu.SemaphoreType.DMA,
            pltpu.SemaphoreType.DMA,
        ],
    )
    return pl.pallas_call(
        kern,
        grid_spec=grid_spec,
        out_shape=jax.ShapeDtypeStruct((n_rows, LANES), F32),
        compiler_params=pltpu.CompilerParams(
            dimension_semantics=("arbitrary",), vmem_limit_bytes=VMEM_LIMIT_BYTES),
        name="experts",
    )(te, nact, slots_blk, gates_blk, h1rt, w1b, b1, w2b, b2, ln_g, ln_b)


def _key_shift(n, tm):
    return max((n * TOP_K - 1).bit_length(), (N_EXPERTS * tm - 1).bit_length()) + 1


def _routing_plan(keys, counts, n, tm, tbt):
    nb = n // tbt
    tpb = tbt * TOP_K // tm + N_EXPERTS
    shift = _key_shift(n, tm)
    flag = 1 << (shift - 1)
    need = (-counts) % tm
    b_ids = jnp.arange(nb, dtype=jnp.int32)[:, None, None]
    e_ids = jnp.arange(N_EXPERTS, dtype=jnp.int32)[None, :, None]
    j_ids = jnp.arange(tm, dtype=jnp.int32)[None, None, :]
    pad_keys = jnp.where(j_ids < need[:, :, None],
                         (b_ids * GROUP_SLOTS + e_ids) * (1 << shift) + flag + j_ids,
                         (b_ids * GROUP_SLOTS + N_EXPERTS) * (1 << shift) + flag + e_ids * tm + j_ids)
    ks = jnp.sort(jnp.concatenate([keys, pad_keys.reshape(-1)])).reshape(nb, tpb * tm)
    low = ks & (flag - 1)
    real = (ks & flag) == 0
    slot = jnp.where(real, low & (tbt * TOP_K - 1), tbt * TOP_K)
    edge = jnp.full((nb, tm), tbt * TOP_K, jnp.int32)
    slots_blk = jnp.concatenate([edge, slot, edge], axis=1).reshape(nb, 1, (tpb + 2) * tm)
    e_t = (ks[:, ::tm] >> shift) % GROUP_SLOTS
    active = e_t < N_EXPERTS
    te = jnp.where(active, e_t, N_EXPERTS - 1).reshape(-1)
    nact = jnp.sum(active.astype(jnp.int32), axis=1)
    return te, nact, slots_blk


def _pick_tile(total, want):
    t = min(total, want)
    while total % t:
        t //= 2
    return t


def kernel(x, mem, ln_in_g, ln_in_b, mem_ln_g, mem_ln_b, w_in, conv_w, pool_w, pool_scale, gmlp_ln_g, gmlp_ln_b,
           gmlp_w, gmlp_b, w_mem_kv, group_norm_g, w_out, ln1_g, ln1_b, w_router, b_router, w1, b1, w2, b2,
           ln2_g, ln2_b):
    bsz, seq, d = x.shape
    depth = w_in.shape[0]
    n = bsz * seq
    dg = D_GROUP
    ts = _pick_tile(seq, 512)
    tm = EXPERT_TILE
    tbt = _pick_tile(n, EXPERT_BLOCK_TOKENS)
    nb = n // tbt
    assert N_EXPERTS == 32 and tbt % ts == 0 and tbt & (tbt - 1) == 0 and tbt % FINAL_CHUNK == 0
    key_shift = _key_shift(n, tm)
    block_shift = tbt.bit_length() - 1

    h = _ln_rows(x.reshape(n, d), ln_in_g, ln_in_b, _pick_tile(n, 1024))

    wkv_all = jnp.transpose(w_mem_kv, (1, 0, 2)).reshape(d, depth * 2 * dg).astype(BF16)
    kv_all = _mem_kv(mem, mem_ln_g, mem_ln_b, wkv_all)

    pos = jnp.arange(GMLP_BLOCK)
    mask = (pos[None, :] // CHUNK) <= (pos[:, None] // CHUNK)
    eye = jnp.eye(len(POOL_WINDOWS), dtype=F32)

    wr_pad = jnp.zeros((depth, d, LANES), F32).at[:, :, :N_EXPERTS].set(w_router)
    wr_hi = wr_pad.astype(BF16)
    wr_lo = (wr_pad - wr_hi.astype(F32)).astype(BF16)
    br_pad = jnp.full((depth, 1, LANES), NEG_BIG, F32).at[:, 0, :N_EXPERTS].set(b_router)
    w1b = w1.astype(BF16)
    w2b = w2.astype(BF16)
    b1r = b1.reshape(depth, N_EXPERTS, 1, 2 * D_FF)
    b2r = b2.reshape(depth, N_EXPERTS, 1, d)

    for l in range(depth):
        pool_bd = jnp.einsum('gh,gcd->gchd', eye, pool_w[l]).reshape(dg, dg).astype(BF16)
        gw = jnp.where(mask[None], gmlp_w[l], 0.0).reshape(GMLP_HEADS * GMLP_BLOCK, GMLP_BLOCK).astype(BF16)
        gbias = jnp.repeat(gmlp_b[l].T, GMLP_HEAD_DIM, axis=1)
        lw = (w_in[l].astype(BF16), conv_w[l], pool_bd, pool_scale[l].reshape(1, dg),
              gmlp_ln_g[l].reshape(1, dg), gmlp_ln_b[l].reshape(1, dg), gw, gbias,
              group_norm_g[l].reshape(1, d), w_out[l].astype(BF16),
              ln1_g[l].reshape(1, d), ln1_b[l].reshape(1, d), wr_hi[l], wr_lo[l], br_pad[l])
        h1rt, keys, gates, cnt = _mixer(h, kv_all, l, lw, bsz=bsz, seq=seq, ts=ts,
                                        key_shift=key_shift, block_shift=block_shift)
        counts = jnp.sum(cnt.reshape(nb, tbt // ts, LANES), axis=1)[:, :N_EXPERTS].astype(jnp.int32)
        te, nact, slots_blk = _routing_plan(keys[:, :TOP_K].reshape(-1), counts, n, tm, tbt)
        gates_blk = jnp.pad(gates[:, :TOP_K].reshape(nb, 1, tbt * TOP_K), ((0, 0), (0, 0), (0, LANES)))
        h = _experts(h1rt, te, nact, slots_blk, gates_blk, w1b, b1r, w2b, b2r,
                     ln2_g[l].reshape(ROW_TILE_ROWS, LANES), ln2_b[l].reshape(ROW_TILE_ROWS, LANES), l,
                     tm=tm, tbt=tbt)
    return h.reshape(bsz, seq, d)
```

```python
import functools

import jax
import jax.numpy as jnp
from jax import lax
from jax.experimental import pallas as pl
from jax.experimental.pallas import tpu as pltpu

F32 = jnp.float32
BF16 = jnp.bfloat16

D_MODEL = 1024
N_GROUPS = 4
D_GROUP = D_MODEL // N_GROUPS
D_IN_PROJ = 7 * D_GROUP
POOL_WINDOWS = (2, 4, 8, 16)
POOL_CH = D_GROUP // len(POOL_WINDOWS)
GMLP_BLOCK = 128
GMLP_HEADS = 4
GMLP_HEAD_DIM = D_GROUP // GMLP_HEADS
CHUNK = 64
MEM_HEADS = 4
MEM_HEAD_DIM = D_GROUP // MEM_HEADS
N_EXPERTS = 32
GROUP_SLOTS = N_EXPERTS + 1
TOP_K = 4
D_FF = D_MODEL
SWIGLU_ALPHA = 1.702
SWIGLU_LIMIT = 7.0
REF_DEPTH = 4
DEEPNORM_ALPHA = (2 * REF_DEPTH) ** 0.25
LN_EPS = 1e-5
RMS_EPS = 1e-6

SUBLANES = 8
LANES = 128
ROW_TILE_ROWS = D_MODEL // LANES
HALO = max(POOL_WINDOWS)
VMEM_LIMIT_BYTES = 58 * 1024 * 1024
EXPERT_CHUNKS = 4
EXPERT_TILE = 256
EXPERT_BLOCK_TOKENS = 4096
STAGE_PITCH = EXPERT_TILE + SUBLANES
FINAL_CHUNK = 128
ROW_BATCH = 8

NEG_BIG = -1e30


def _layer_norm(x, g, b):
    mu = jnp.mean(x, axis=-1, keepdims=True)
    xc = x - mu
    var = jnp.mean(xc * xc, axis=-1, keepdims=True)
    return xc * lax.rsqrt(var + LN_EPS) * g + b


def _ln_kernel(x_ref, g_ref, b_ref, o_ref, *, rows):
    y = _layer_norm(x_ref[...], g_ref[...], b_ref[...])
    for j in range(ROW_TILE_ROWS):
        o_ref[pl.ds(j, rows, stride=ROW_TILE_ROWS), :] = y[:, j * LANES:(j + 1) * LANES]


def _ln_rows(x, g, b, rows):
    n, d = x.shape
    return pl.pallas_call(
        functools.partial(_ln_kernel, rows=rows),
        grid=(n // rows,),
        in_specs=[pl.BlockSpec((rows, d), lambda i: (i, 0)),
                  pl.BlockSpec((1, d), lambda i: (0, 0)),
                  pl.BlockSpec((1, d), lambda i: (0, 0))],
        out_specs=pl.BlockSpec((rows * ROW_TILE_ROWS, LANES), lambda i: (i, 0)),
        out_shape=jax.ShapeDtypeStruct((n * ROW_TILE_ROWS, LANES), F32),
        name="entry_ln",
    )(x, g.reshape(1, d), b.reshape(1, d))


def _mem_kv_kernel(m_ref, g_ref, b_ref, w_ref, o_ref):
    mn = _layer_norm(m_ref[0], g_ref[...], b_ref[...])
    o_ref[0] = jnp.dot(mn.astype(BF16), w_ref[...], preferred_element_type=F32).astype(BF16)


def _mem_kv(mem, g, b, w_all):
    bsz, m, d = mem.shape
    p = w_all.shape[1]
    return pl.pallas_call(
        _mem_kv_kernel,
        grid=(bsz,),
        in_specs=[pl.BlockSpec((1, m, d), lambda i: (i, 0, 0)),
                  pl.BlockSpec((1, d), lambda i: (0, 0)),
                  pl.BlockSpec((1, d), lambda i: (0, 0)),
                  pl.BlockSpec((d, p), lambda i: (0, 0))],
        out_specs=pl.BlockSpec((1, m, p), lambda i: (i, 0, 0)),
        out_shape=jax.ShapeDtypeStruct((bsz, m, p), BF16),
        name="mem_kv",
    )(mem, g.reshape(1, d), b.reshape(1, d), w_all)


def _mixer_kernel(x_ref, kv_ref, win_ref, convw_ref, poolw_ref, pscale_ref, glng_ref, glnb_ref,
                  gw_ref, gbias_ref, gng_ref, wout_ref, ln1g_ref, ln1b_ref, wrh_ref, wrl_ref, br_ref,
                  h1rt_ref, key_ref, gate_ref, cnt_ref, ext_ref, *, ts, key_shift, block_shift):
    s = pl.program_id(1)
    dg = D_GROUP

    x = jnp.concatenate([x_ref[pl.ds(j, ts, stride=ROW_TILE_ROWS), :] for j in range(ROW_TILE_ROWS)], axis=1)
    proj = jnp.dot(x.astype(BF16), win_ref[...], preferred_element_type=F32)
    gate_b = proj[:, 0:dg]
    gate_c = proj[:, dg:2 * dg]
    conv_in = proj[:, 2 * dg:3 * dg]
    pool_in = proj[:, 3 * dg:4 * dg]
    gmlp_in = proj[:, 4 * dg:6 * dg]
    mem_q = proj[:, 6 * dg:7 * dg]

    @pl.when(s == 0)
    def _():
        ext_ref[0:HALO, :] = jnp.zeros((HALO, 2 * dg), F32)

    z = gate_c * conv_in
    ext_ref[HALO:HALO + ts, 0:dg] = z
    ext_ref[HALO:HALO + ts, dg:2 * dg] = pool_in

    zm1 = ext_ref[pl.ds(HALO - 1, ts), 0:dg]
    zm2 = ext_ref[pl.ds(HALO - 2, ts), 0:dg]
    cw = convw_ref[...]
    y_conv = gate_b * (cw[0:1, :] * zm2 + cw[1:2, :] * zm1 + cw[2:3, :] * z)

    pos = s * ts + lax.broadcasted_iota(jnp.int32, (ts, 1), 0)
    half = dg // 2
    lane_h = lax.broadcasted_iota(jnp.int32, (ts, half), 1)
    diffs = []
    for hf in range(2):
        c0 = dg + hf * half
        w_lo, w_hi = POOL_WINDOWS[2 * hf], POOL_WINDOWS[2 * hf + 1]
        cur = ext_ref[pl.ds(HALO, ts), c0:c0 + half]
        acc = cur
        for k in range(1, w_lo):
            acc = acc + ext_ref[pl.ds(HALO - k, ts), c0:c0 + half]
        s_lo = acc
        for k in range(w_lo, w_hi):
            acc = acc + ext_ref[pl.ds(HALO - k, ts), c0:c0 + half]
        s_hi = acc
        cnt_lo = jnp.minimum(pos + 1, w_lo).astype(F32)
        cnt_hi = jnp.minimum(pos + 1, w_hi).astype(F32)
        mean = jnp.where(lane_h < POOL_CH, s_lo / cnt_lo, s_hi / cnt_hi)
        diffs.append(mean - cur)
    dpool = jnp.concatenate(diffs, axis=1)
    y_pool = jnp.dot(dpool.astype(BF16), poolw_ref[...], preferred_element_type=F32) * pscale_ref[...]

    ext_ref[0:HALO, :] = ext_ref[ts:ts + HALO, :]

    zg = jax.nn.gelu(gmlp_in, approximate=True)
    u = zg[:, 0:dg]
    v = _layer_norm(zg[:, dg:2 * dg], glng_ref[...], glnb_ref[...]).astype(BF16)
    lane_g = lax.broadcasted_iota(jnp.int32, (GMLP_BLOCK, dg), 1)
    gw = gw_ref[...]
    gbias = gbias_ref[...]
    gates = []
    for nb in range(ts // GMLP_BLOCK):
        vb = v[nb * GMLP_BLOCK:(nb + 1) * GMLP_BLOCK, :]
        r = jnp.dot(gw, vb, preferred_element_type=F32)
        g = r[0:GMLP_BLOCK, :]
        for h in range(1, GMLP_HEADS):
            g = jnp.where(lane_g >= h * GMLP_HEAD_DIM, r[h * GMLP_BLOCK:(h + 1) * GMLP_BLOCK, :], g)
        gates.append(g + gbias)
    y_gmlp = u * jnp.concatenate(gates, axis=0)

    kv = kv_ref[0]
    mk = kv[:, 0:dg]
    mv = kv[:, dg:2 * dg]
    lane_q = lax.broadcasted_iota(jnp.int32, (ts, dg), 1)
    qs = mem_q * (MEM_HEAD_DIM ** -0.5)
    y_mem = jnp.zeros((ts, dg), F32)
    for h in range(MEM_HEADS):
        in_head = (lane_q >= h * MEM_HEAD_DIM) & (lane_q < (h + 1) * MEM_HEAD_DIM)
        qh = jnp.where(in_head, qs, 0.0).astype(BF16)
        sc = lax.dot_general(qh, mk, (((1,), (1,)), ((), ())), preferred_element_type=F32)
        e = jnp.exp(sc - jnp.max(sc, axis=-1, keepdims=True))
        den = jnp.sum(e, axis=-1, keepdims=True)
        oh = jnp.dot(e.astype(BF16), mv, preferred_element_type=F32) / den
        y_mem = jnp.where(in_head, oh, y_mem)

    gng = gng_ref[...]
    groups = []
    for gi, y in enumerate((y_conv, y_pool, y_gmlp, y_mem)):
        ms = jnp.mean(y * y, axis=-1, keepdims=True)
        groups.append(y * lax.rsqrt(ms + RMS_EPS) * gng[:, gi * dg:(gi + 1) * dg])
    cat = jnp.concatenate(groups, axis=1).astype(BF16)
    mix = jnp.dot(cat, wout_ref[...], preferred_element_type=F32)

    h1 = _layer_norm(DEEPNORM_ALPHA * x + mix, ln1g_ref[...], ln1b_ref[...])
    for j in range(ROW_TILE_ROWS):
        h1rt_ref[pl.ds(j, ts, stride=ROW_TILE_ROWS), :] = h1[:, j * LANES:(j + 1) * LANES]

    h1_hi = h1.astype(BF16)
    h1_lo = (h1 - h1_hi.astype(F32)).astype(BF16)
    logits = (jnp.dot(h1_hi, wrh_ref[...], preferred_element_type=F32)
              + jnp.dot(h1_lo, wrh_ref[...], preferred_element_type=F32)
              + jnp.dot(h1_hi, wrl_ref[...], preferred_element_type=F32)) + br_ref[...]
    lane = lax.broadcasted_iota(jnp.int32, (ts, LANES), 1)
    lane_f = lane.astype(F32)
    vals, idxs = [], []
    cur_l = logits
    for _ in range(TOP_K):
        m = jnp.max(cur_l, axis=-1, keepdims=True)
        i_f = jnp.min(jnp.where(cur_l == m, lane_f, float(LANES)), axis=-1, keepdims=True)
        vals.append(m)
        idxs.append(i_f)
        cur_l = jnp.where(lane_f == i_f, -jnp.inf, cur_l)
    es = [jnp.exp(vk - vals[0]) for vk in vals]
    den = es[0] + es[1] + es[2] + es[3]
    idx_out = jnp.zeros((ts, LANES), F32)
    gate_out = jnp.zeros((ts, LANES), F32)
    hits = jnp.zeros((ts, LANES), F32)
    for k in range(TOP_K):
        idx_out = jnp.where(lane == k, idxs[k], idx_out)
        gate_out = jnp.where(lane == k, es[k] / den, gate_out)
        hits = hits + jnp.where(lane_f == idxs[k], 1.0, 0.0)
    tok = (pl.program_id(0) * pl.num_programs(1) + s) * ts + lax.broadcasted_iota(jnp.int32, (ts, LANES), 0)
    group = (tok >> block_shift) * GROUP_SLOTS + idx_out.astype(jnp.int32)
    key_ref[...] = jnp.where(lane < TOP_K, group * (1 << key_shift) + tok * TOP_K + lane, 0)
    gate_ref[...] = gate_out
    cnt_ref[0] = jnp.sum(hits, axis=0, keepdims=True)


def _mixer(h, kv_all, layer, lw, *, bsz, seq, ts, key_shift, block_shift):
    n = bsz * seq
    st = seq // ts
    dg = D_GROUP
    const = lambda b, s: (0, 0)
    kern = functools.partial(_mixer_kernel, ts=ts, key_shift=key_shift, block_shift=block_shift)
    return pl.pallas_call(
        kern,
        grid=(bsz, st),
        in_specs=[
            pl.BlockSpec((ts * ROW_TILE_ROWS, LANES), lambda b, s: (b * st + s, 0)),
            pl.BlockSpec((1, kv_all.shape[1], 2 * dg), lambda b, s: (b, 0, layer)),
            pl.BlockSpec((D_MODEL, D_IN_PROJ), const),
            pl.BlockSpec((3, dg), const),
            pl.BlockSpec((dg, dg), const),
            pl.BlockSpec((1, dg), const),
            pl.BlockSpec((1, dg), const),
            pl.BlockSpec((1, dg), const),
            pl.BlockSpec((GMLP_HEADS * GMLP_BLOCK, GMLP_BLOCK), const),
            pl.BlockSpec((GMLP_BLOCK, dg), const),
            pl.BlockSpec((1, D_MODEL), const),
            pl.BlockSpec((D_MODEL, D_MODEL), const),
            pl.BlockSpec((1, D_MODEL), const),
            pl.BlockSpec((1, D_MODEL), const),
            pl.BlockSpec((D_MODEL, LANES), const),
            pl.BlockSpec((D_MODEL, LANES), const),
            pl.BlockSpec((1, LANES), const),
        ],
        out_specs=[
            pl.BlockSpec((ts * ROW_TILE_ROWS, LANES), lambda b, s: (b * st + s, 0)),
            pl.BlockSpec((ts, LANES), lambda b, s: (b * st + s, 0)),
            pl.BlockSpec((ts, LANES), lambda b, s: (b * st + s, 0)),
            pl.BlockSpec((1, 1, LANES), lambda b, s: (b * st + s, 0, 0)),
        ],
        out_shape=[
            jax.ShapeDtypeStruct((n * ROW_TILE_ROWS, LANES), F32),
            jax.ShapeDtypeStruct((n, LANES), jnp.int32),
            jax.ShapeDtypeStruct((n, LANES), F32),
            jax.ShapeDtypeStruct((bsz * st, 1, LANES), F32),
        ],
        scratch_shapes=[pltpu.VMEM((ts + HALO, 2 * dg), F32)],
        compiler_params=pltpu.CompilerParams(
            dimension_semantics=("arbitrary", "arbitrary"), vmem_limit_bytes=VMEM_LIMIT_BYTES),
        name="mixer",
    )(h, kv_all, *lw)


def _expert_kernel(te_ref, nact_ref, winfo_ref, slot_ref, gate_ref,
                   h_hbm, w1_hbm, b1_ref, w2_hbm, b2_ref, g_ref, bt_ref, o_hbm,
                   xblk, ybuf, sin, sout, hbuf, w1buf, w2buf, xsem, osem, wsem, *, tm, tbt, tpb, layer):
    i = pl.program_id(0)
    b = i // tpb
    tj = i - b * tpb
    n_act = nact_ref[b]
    par = lax.rem(i, 2)
    rt = ROW_TILE_ROWS
    blk_rows = tbt * rt
    sp = STAGE_PITCH
    cw = D_FF // EXPERT_CHUNKS
    base = (tj + 1) * tm

    def zero_row(tiles, width):
        bits = pltpu.bitcast(tiles[0], jnp.uint32)
        for t in tiles[1:]:
            bits = bits | pltpu.bitcast(t, jnp.uint32)
        z = ((bits >> 16) >> 16).astype(F32)[0:1, :]
        return jnp.concatenate([z] * (width // LANES), axis=1)

    def gather(first_entry, dst, lo=0, hi=tm):
        tiles = []
        entries = slot_ref.at[0, 0, pl.ds(first_entry, tm)]
        for r in range(lo, hi):
            off = pl.multiple_of((entries[r] >> 2) * rt, rt)
            tiles.append(xblk[pl.ds(off, rt), :])
            dst[pl.ds(r * rt, rt), :] = tiles[-1]
        return tiles

    def accumulate(first_entry, src, lo=0, hi=tm):
        sums = []
        entries = slot_ref.at[0, 0, pl.ds(first_entry, tm)]
        for r0 in range(lo, hi, ROW_BATCH):
            rs = range(r0, r0 + ROW_BATCH)
            slots = [entries[r] for r in rs]
            offs = [pl.multiple_of((s >> 2) * rt, rt) for s in slots]
            accs = [ybuf[pl.ds(o, rt), :] for o in offs]
            vals = [src[pl.ds(r, rt, stride=sp), :] for r in rs]
            for k, r in enumerate(rs):
                gate = gate_ref[0, 0, slots[k]]
                sums.append(accs[k] + gate * vals[k])
                ybuf[pl.ds(offs[k], rt), :] = sums[-1]
        return sums

    @pl.when(i == 0)
    def _():
        xblk[pl.ds(blk_rows, rt), :] = jnp.zeros((rt, LANES), F32)
        sout[...] = jnp.zeros(sout.shape, F32)

    def weight_copies(e, wslot):
        return (pltpu.make_async_copy(w1_hbm.at[layer, e], w1buf.at[wslot], wsem.at[wslot]),
                pltpu.make_async_copy(w2_hbm.at[layer, e], w2buf.at[wslot], wsem.at[wslot]))

    @pl.when(tj < n_act)
    def _():
        first = tj == 0
        last = tj == n_act - 1
        row0 = pl.multiple_of(b * blk_rows, blk_rows)
        info = winfo_ref[i]
        wpar = (info >> 2) & 1

        @pl.when(i == 0)
        def _():
            for cp in weight_copies(te_ref[0], 0):
                cp.start()

        @pl.when((info & 1) == 1)
        def _():
            for cp in weight_copies(te_ref[i], wpar):
                cp.wait()

            @pl.when((info & 2) == 2)
            def _():
                for cp in weight_copies(info >> 3, 1 - wpar):
                    cp.start()

        w1_ref = w1buf.at[wpar]
        w2_ref = w2buf.at[wpar]

        @pl.when(first)
        def _():
            load = pltpu.make_async_copy(h_hbm.at[pl.ds(row0, blk_rows), :], xblk.at[pl.ds(0, blk_rows), :], xsem)
            load.start()
            zrows = FINAL_CHUNK * rt

            def zero(c, carry):
                ybuf[pl.ds(pl.multiple_of(c * zrows, zrows), zrows), :] = jnp.zeros((zrows, LANES), F32)
                return carry

            lax.fori_loop(0, tbt // FINAL_CHUNK, zero, 0)
            ybuf[pl.ds(blk_rows, rt), :] = jnp.zeros((rt, LANES), F32)
            load.wait()
            gather(base, sin.at[par])

        xs = sin.at[par]
        x = jnp.concatenate([xs[pl.ds(j, tm, stride=rt), :] for j in range(rt)], axis=1)
        xb = x.astype(BF16)

        per1 = tm // (2 * EXPERT_CHUNKS)
        per2 = tm // EXPERT_CHUNKS
        for c in range(2 * EXPERT_CHUNKS):
            tie = zero_row(accumulate(base - tm, sout.at[1 - par], c * per1, (c + 1) * per1), cw)
            hbuf[:, c * cw:(c + 1) * cw] = (jnp.dot(xb, w1_ref[:, c * cw:(c + 1) * cw], preferred_element_type=F32)
                                            + (b1_ref[:, c * cw:(c + 1) * cw] + tie))
        out = None
        for c in range(EXPERT_CHUNKS):
            tie = zero_row(gather(base + tm, sin.at[1 - par], c * per2, (c + 1) * per2), cw)
            glu = jnp.minimum(hbuf[:, c * cw:(c + 1) * cw], SWIGLU_LIMIT)
            lin = jnp.clip(hbuf[:, D_FF + c * cw:D_FF + (c + 1) * cw] + tie, -SWIGLU_LIMIT, SWIGLU_LIMIT)
            act = glu * jax.nn.sigmoid(SWIGLU_ALPHA * glu) * (lin + 1.0)
            part = jnp.dot(act.astype(BF16), w2_ref[c * cw:(c + 1) * cw, :], preferred_element_type=F32)
            out = part if out is None else out + part
        out = out + b2_ref[...]
        so = sout.at[par]
        for j in range(rt):
            so[pl.ds(j * sp, tm), :] = out[:, j * LANES:(j + 1) * LANES]

        @pl.when(last)
        def _():
            accumulate(base, so)
            crows = FINAL_CHUNK * rt
            g3 = g_ref[...].reshape(1, rt, LANES)
            b3 = bt_ref[...].reshape(1, rt, LANES)

            def finish(c, carry):
                rows_c = pl.ds(pl.multiple_of(c * crows, crows), crows)
                r3 = (DEEPNORM_ALPHA * xblk[rows_c, :] + ybuf[rows_c, :]).reshape(FINAL_CHUNK, rt, LANES)
                mu = jnp.sum(jnp.sum(r3, axis=2, keepdims=True), axis=1, keepdims=True) * (1.0 / D_MODEL)
                xc = r3 - mu
                var = jnp.sum(jnp.sum(xc * xc, axis=2, keepdims=True), axis=1, keepdims=True) * (1.0 / D_MODEL)
                y3 = xc * lax.rsqrt(var + LN_EPS) * g3 + b3
                ybuf[rows_c, :] = y3.reshape(crows, LANES)
                pltpu.make_async_copy(
                    ybuf.at[rows_c, :],
                    o_hbm.at[pl.ds(pl.multiple_of(row0 + c * crows, crows), crows), :], osem).start()
                return carry

            lax.fori_loop(0, tbt // FINAL_CHUNK, finish, 0)
            pltpu.make_async_copy(ybuf.at[pl.ds(0, blk_rows), :], o_hbm.at[pl.ds(row0, blk_rows), :], osem).wait()


def _experts(h1rt, te, nact, winfo, slots_blk, gates_blk, w1b, b1, w2b, b2, ln_g, ln_b, layer, *, tm, tbt):
    nt = te.shape[0]
    tpb = nt // nact.shape[0]
    n_rows = h1rt.shape[0]
    kern = functools.partial(_expert_kernel, tm=tm, tbt=tbt, tpb=tpb, layer=layer)
    per_block = lambda i, te, na, wi: (i // tpb, 0, 0)
    per_expert = lambda i, te, na, wi: (layer, te[i], 0, 0)
    const = lambda i, te, na, wi: (0, 0)
    grid_spec = pltpu.PrefetchScalarGridSpec(
        num_scalar_prefetch=3,
        grid=(nt,),
        in_specs=[
            pl.BlockSpec((1, 1, slots_blk.shape[2]), per_block, memory_space=pltpu.SMEM,
                         pipeline_mode=pl.Buffered(1)),
            pl.BlockSpec((1, 1, gates_blk.shape[2]), per_block, memory_space=pltpu.SMEM,
                         pipeline_mode=pl.Buffered(1)),
            pl.BlockSpec(memory_space=pl.ANY),
            pl.BlockSpec(memory_space=pl.ANY),
            pl.BlockSpec((None, None, 1, 2 * D_FF), per_expert),
            pl.BlockSpec(memory_space=pl.ANY),
            pl.BlockSpec((None, None, 1, D_MODEL), per_expert),
            pl.BlockSpec((ROW_TILE_ROWS, LANES), const),
            pl.BlockSpec((ROW_TILE_ROWS, LANES), const),
        ],
        out_specs=pl.BlockSpec(memory_space=pl.ANY),
        scratch_shapes=[
            pltpu.VMEM(((tbt + 1) * ROW_TILE_ROWS, LANES), F32),
            pltpu.VMEM(((tbt + 1) * ROW_TILE_ROWS, LANES), F32),
            pltpu.VMEM((2, tm * ROW_TILE_ROWS, LANES), F32),
            pltpu.VMEM((2, ROW_TILE_ROWS * STAGE_PITCH, LANES), F32),
            pltpu.VMEM((tm, 2 * D_FF), F32),
            pltpu.VMEM((2, D_MODEL, 2 * D_FF), BF16),
            pltpu.VMEM((2, D_FF, D_MODEL), BF16),
            pltpu.SemaphoreType.DMA,
            pltpu.SemaphoreType.DMA,
            pltpu.SemaphoreType.DMA((2,)),
        ],
    )
    return pl.pallas_call(
        kern,
        grid_spec=grid_spec,
        out_shape=jax.ShapeDtypeStruct((n_rows, LANES), F32),
        compiler_params=pltpu.CompilerParams(
            dimension_semantics=("arbitrary",), vmem_limit_bytes=VMEM_LIMIT_BYTES),
        name="experts",
    )(te, nact, winfo, slots_blk, gates_blk, h1rt, w1b, b1, w2b, b2, ln_g, ln_b)


def _key_shift(n, tm):
    return max((n * TOP_K - 1).bit_length(), (N_EXPERTS * tm - 1).bit_length()) + 1


def _routing_plan(keys, counts, n, tm, tbt):
    nb = n // tbt
    tpb = tbt * TOP_K // tm + N_EXPERTS
    shift = _key_shift(n, tm)
    flag = 1 << (shift - 1)
    need = (-counts) % tm
    b_ids = jnp.arange(nb, dtype=jnp.int32)[:, None, None]
    e_ids = jnp.arange(N_EXPERTS, dtype=jnp.int32)[None, :, None]
    j_ids = jnp.arange(tm, dtype=jnp.int32)[None, None, :]
    pad_keys = jnp.where(j_ids < need[:, :, None],
                         (b_ids * GROUP_SLOTS + e_ids) * (1 << shift) + flag + j_ids,
                         (b_ids * GROUP_SLOTS + N_EXPERTS) * (1 << shift) + flag + e_ids * tm + j_ids)
    ks = jnp.sort(jnp.concatenate([keys, pad_keys.reshape(-1)])).reshape(nb, tpb * tm)
    low = ks & (flag - 1)
    real = (ks & flag) == 0
    slot = jnp.where(real, low & (tbt * TOP_K - 1), tbt * TOP_K)
    edge = jnp.full((nb, tm), tbt * TOP_K, jnp.int32)
    slots_blk = jnp.concatenate([edge, slot, edge], axis=1).reshape(nb, 1, (tpb + 2) * tm)
    e_t = (ks[:, ::tm] >> shift) % GROUP_SLOTS
    active = e_t < N_EXPERTS
    nact = jnp.sum(active.astype(jnp.int32), axis=1)
    active = active.reshape(-1)
    te = jnp.where(active, e_t.reshape(-1), N_EXPERTS - 1)
    nt = nb * tpb
    tile = jnp.arange(nt, dtype=jnp.int32)
    prev_same = (tile % tpb != 0) & (jnp.roll(te, 1) == te) & jnp.roll(active, 1)
    gstart = active & ~prev_same
    gidx = jnp.cumsum(gstart.astype(jnp.int32)) - 1
    starts = jnp.where(gstart, tile, nt)
    nxt = jnp.flip(lax.cummin(jnp.flip(jnp.roll(starts, -1).at[-1].set(nt))))
    has_next = nxt < nt
    tnext = te[jnp.minimum(nxt, nt - 1)]
    winfo = (gstart.astype(jnp.int32) | (has_next.astype(jnp.int32) << 1) | ((gidx & 1) << 2) | (tnext << 3))
    return te, nact, winfo, slots_blk


def _pick_tile(total, want):
    t = min(total, want)
    while total % t:
        t //= 2
    return t


def kernel(x, mem, ln_in_g, ln_in_b, mem_ln_g, mem_ln_b, w_in, conv_w, pool_w, pool_scale, gmlp_ln_g, gmlp_ln_b,
           gmlp_w, gmlp_b, w_mem_kv, group_norm_g, w_out, ln1_g, ln1_b, w_router, b_router, w1, b1, w2, b2,
           ln2_g, ln2_b):
    bsz, seq, d = x.shape
    depth = w_in.shape[0]
    n = bsz * seq
    dg = D_GROUP
    ts = _pick_tile(seq, 512)
    tm = EXPERT_TILE
    tbt = _pick_tile(n, EXPERT_BLOCK_TOKENS)
    nb = n // tbt
    assert N_EXPERTS == 32 and tbt % ts == 0 and tbt & (tbt - 1) == 0 and tbt % FINAL_CHUNK == 0
    key_shift = _key_shift(n, tm)
    block_shift = tbt.bit_length() - 1

    h = _ln_rows(x.reshape(n, d), ln_in_g, ln_in_b, _pick_tile(n, 1024))

    wkv_all = jnp.transpose(w_mem_kv, (1, 0, 2)).reshape(d, depth * 2 * dg).astype(BF16)
    kv_all = _mem_kv(mem, mem_ln_g, mem_ln_b, wkv_all)

    pos = jnp.arange(GMLP_BLOCK)
    mask = (pos[None, :] // CHUNK) <= (pos[:, None] // CHUNK)
    eye = jnp.eye(len(POOL_WINDOWS), dtype=F32)

    wr_pad = jnp.zeros((depth, d, LANES), F32).at[:, :, :N_EXPERTS].set(w_router)
    wr_hi = wr_pad.astype(BF16)
    wr_lo = (wr_pad - wr_hi.astype(F32)).astype(BF16)
    br_pad = jnp.full((depth, 1, LANES), NEG_BIG, F32).at[:, 0, :N_EXPERTS].set(b_router)
    w1b = w1.astype(BF16)
    w2b = w2.astype(BF16)
    b1r = b1.reshape(depth, N_EXPERTS, 1, 2 * D_FF)
    b2r = b2.reshape(depth, N_EXPERTS, 1, d)

    for l in range(depth):
        pool_bd = jnp.einsum('gh,gcd->gchd', eye, pool_w[l]).reshape(dg, dg).astype(BF16)
        gw = jnp.where(mask[None], gmlp_w[l], 0.0).reshape(GMLP_HEADS * GMLP_BLOCK, GMLP_BLOCK).astype(BF16)
        gbias = jnp.repeat(gmlp_b[l].T, GMLP_HEAD_DIM, axis=1)
        lw = (w_in[l].astype(BF16), conv_w[l], pool_bd, pool_scale[l].reshape(1, dg),
              gmlp_ln_g[l].reshape(1, dg), gmlp_ln_b[l].reshape(1, dg), gw, gbias,
              group_norm_g[l].reshape(1, d), w_out[l].astype(BF16),
              ln1_g[l].reshape(1, d), ln1_b[l].reshape(1, d), wr_hi[l], wr_lo[l], br_pad[l])
        h1rt, keys, gates, cnt = _mixer(h, kv_all, l, lw, bsz=bsz, seq=seq, ts=ts,
                                        key_shift=key_shift, block_shift=block_shift)
        counts = jnp.sum(cnt.reshape(nb, tbt // ts, LANES), axis=1)[:, :N_EXPERTS].astype(jnp.int32)
        te, nact, winfo, slots_blk = _routing_plan(keys[:, :TOP_K].reshape(-1), counts, n, tm, tbt)
        gates_blk = jnp.pad(gates[:, :TOP_K].reshape(nb, 1, tbt * TOP_K), ((0, 0), (0, 0), (0, LANES)))
        h = _experts(h1rt, te, nact, winfo, slots_blk, gates_blk, w1b, b1r, w2b, b2r,
                     ln2_g[l].reshape(ROW_TILE_ROWS, LANES), ln2_b[l].reshape(ROW_TILE_ROWS, LANES), l,
                     tm=tm, tbt=tbt)
    return h.reshape(bsz, seq, d)
```

```python
import functools

import jax
import jax.numpy as jnp
from jax import lax
from jax.experimental import pallas as pl
from jax.experimental.pallas import tpu as pltpu

F32 = jnp.float32
BF16 = jnp.bfloat16

D_MODEL = 1024
N_GROUPS = 4
D_GROUP = D_MODEL // N_GROUPS
D_IN_PROJ = 7 * D_GROUP
POOL_WINDOWS = (2, 4, 8, 16)
POOL_CH = D_GROUP // len(POOL_WINDOWS)
GMLP_BLOCK = 128
GMLP_HEADS = 4
GMLP_HEAD_DIM = D_GROUP // GMLP_HEADS
CHUNK = 64
MEM_HEADS = 4
MEM_HEAD_DIM = D_GROUP // MEM_HEADS
N_EXPERTS = 32
GROUP_SLOTS = N_EXPERTS + 1
TOP_K = 4
D_FF = D_MODEL
SWIGLU_ALPHA = 1.702
SWIGLU_LIMIT = 7.0
REF_DEPTH = 4
DEEPNORM_ALPHA = (2 * REF_DEPTH) ** 0.25
LN_EPS = 1e-5
RMS_EPS = 1e-6

SUBLANES = 8
LANES = 128
ROW_TILE_ROWS = D_MODEL // LANES
HALO = max(POOL_WINDOWS)
VMEM_LIMIT_BYTES = 58 * 1024 * 1024
EXPERT_CHUNKS = 4
EXPERT_TILE = 256
EXPERT_BLOCK_TOKENS = 4096
STAGE_PITCH = EXPERT_TILE + SUBLANES
FINAL_CHUNK = 128
ROW_BATCH = 8
WEIGHT_DMA_PARTS = 4

NEG_BIG = -1e30


def _layer_norm(x, g, b):
    mu = jnp.mean(x, axis=-1, keepdims=True)
    xc = x - mu
    var = jnp.mean(xc * xc, axis=-1, keepdims=True)
    return xc * lax.rsqrt(var + LN_EPS) * g + b


def _ln_kernel(x_ref, g_ref, b_ref, o_ref, *, rows):
    y = _layer_norm(x_ref[...], g_ref[...], b_ref[...])
    for j in range(ROW_TILE_ROWS):
        o_ref[pl.ds(j, rows, stride=ROW_TILE_ROWS), :] = y[:, j * LANES:(j + 1) * LANES]


def _ln_rows(x, g, b, rows):
    n, d = x.shape
    return pl.pallas_call(
        functools.partial(_ln_kernel, rows=rows),
        grid=(n // rows,),
        in_specs=[pl.BlockSpec((rows, d), lambda i: (i, 0)),
                  pl.BlockSpec((1, d), lambda i: (0, 0)),
                  pl.BlockSpec((1, d), lambda i: (0, 0))],
        out_specs=pl.BlockSpec((rows * ROW_TILE_ROWS, LANES), lambda i: (i, 0)),
        out_shape=jax.ShapeDtypeStruct((n * ROW_TILE_ROWS, LANES), F32),
        name="entry_ln",
    )(x, g.reshape(1, d), b.reshape(1, d))


def _mem_kv_kernel(m_ref, g_ref, b_ref, w_ref, o_ref):
    mn = _layer_norm(m_ref[0], g_ref[...], b_ref[...])
    o_ref[0] = jnp.dot(mn.astype(BF16), w_ref[...], preferred_element_type=F32).astype(BF16)


def _mem_kv(mem, g, b, w_all):
    bsz, m, d = mem.shape
    p = w_all.shape[1]
    return pl.pallas_call(
        _mem_kv_kernel,
        grid=(bsz,),
        in_specs=[pl.BlockSpec((1, m, d), lambda i: (i, 0, 0)),
                  pl.BlockSpec((1, d), lambda i: (0, 0)),
                  pl.BlockSpec((1, d), lambda i: (0, 0)),
                  pl.BlockSpec((d, p), lambda i: (0, 0))],
        out_specs=pl.BlockSpec((1, m, p), lambda i: (i, 0, 0)),
        out_shape=jax.ShapeDtypeStruct((bsz, m, p), BF16),
        name="mem_kv",
    )(mem, g.reshape(1, d), b.reshape(1, d), w_all)


def _mixer_kernel(x_ref, kv_ref, win_ref, convw_ref, poolw_ref, pscale_ref, glng_ref, glnb_ref,
                  gw_ref, gbias_ref, gng_ref, wout_ref, ln1g_ref, ln1b_ref, wrh_ref, wrl_ref, br_ref,
                  h1rt_ref, key_ref, gate_ref, cnt_ref, ext_ref, *, ts, key_shift, block_shift):
    s = pl.program_id(1)
    dg = D_GROUP

    x = jnp.concatenate([x_ref[pl.ds(j, ts, stride=ROW_TILE_ROWS), :] for j in range(ROW_TILE_ROWS)], axis=1)
    proj = jnp.dot(x.astype(BF16), win_ref[...], preferred_element_type=F32)
    gate_b = proj[:, 0:dg]
    gate_c = proj[:, dg:2 * dg]
    conv_in = proj[:, 2 * dg:3 * dg]
    pool_in = proj[:, 3 * dg:4 * dg]
    gmlp_in = proj[:, 4 * dg:6 * dg]
    mem_q = proj[:, 6 * dg:7 * dg]

    @pl.when(s == 0)
    def _():
        ext_ref[0:HALO, :] = jnp.zeros((HALO, 2 * dg), F32)

    z = gate_c * conv_in
    ext_ref[HALO:HALO + ts, 0:dg] = z
    ext_ref[HALO:HALO + ts, dg:2 * dg] = pool_in

    zm1 = ext_ref[pl.ds(HALO - 1, ts), 0:dg]
    zm2 = ext_ref[pl.ds(HALO - 2, ts), 0:dg]
    cw = convw_ref[...]
    y_conv = gate_b * (cw[0:1, :] * zm2 + cw[1:2, :] * zm1 + cw[2:3, :] * z)

    pos = s * ts + lax.broadcasted_iota(jnp.int32, (ts, 1), 0)
    half = dg // 2
    lane_h = lax.broadcasted_iota(jnp.int32, (ts, half), 1)
    diffs = []
    for hf in range(2):
        c0 = dg + hf * half
        w_lo, w_hi = POOL_WINDOWS[2 * hf], POOL_WINDOWS[2 * hf + 1]
        cur = ext_ref[pl.ds(HALO, ts), c0:c0 + half]
        acc = cur
        for k in range(1, w_lo):
            acc = acc + ext_ref[pl.ds(HALO - k, ts), c0:c0 + half]
        s_lo = acc
        for k in range(w_lo, w_hi):
            acc = acc + ext_ref[pl.ds(HALO - k, ts), c0:c0 + half]
        s_hi = acc
        cnt_lo = jnp.minimum(pos + 1, w_lo).astype(F32)
        cnt_hi = jnp.minimum(pos + 1, w_hi).astype(F32)
        mean = jnp.where(lane_h < POOL_CH, s_lo / cnt_lo, s_hi / cnt_hi)
        diffs.append(mean - cur)
    dpool = jnp.concatenate(diffs, axis=1)
    y_pool = jnp.dot(dpool.astype(BF16), poolw_ref[...], preferred_element_type=F32) * pscale_ref[...]

    ext_ref[0:HALO, :] = ext_ref[ts:ts + HALO, :]

    zg = jax.nn.gelu(gmlp_in, approximate=True)
    u = zg[:, 0:dg]
    v = _layer_norm(zg[:, dg:2 * dg], glng_ref[...], glnb_ref[...]).astype(BF16)
    lane_g = lax.broadcasted_iota(jnp.int32, (GMLP_BLOCK, dg), 1)
    gw = gw_ref[...]
    gbias = gbias_ref[...]
    gates = []
    for nb in range(ts // GMLP_BLOCK):
        vb = v[nb * GMLP_BLOCK:(nb + 1) * GMLP_BLOCK, :]
        r = jnp.dot(gw, vb, preferred_element_type=F32)
        g = r[0:GMLP_BLOCK, :]
        for h in range(1, GMLP_HEADS):
            g = jnp.where(lane_g >= h * GMLP_HEAD_DIM, r[h * GMLP_BLOCK:(h + 1) * GMLP_BLOCK, :], g)
        gates.append(g + gbias)
    y_gmlp = u * jnp.concatenate(gates, axis=0)

    kv = kv_ref[0]
    mk = kv[:, 0:dg]
    mv = kv[:, dg:2 * dg]
    lane_q = lax.broadcasted_iota(jnp.int32, (ts, dg), 1)
    qs = mem_q * (MEM_HEAD_DIM ** -0.5)
    y_mem = jnp.zeros((ts, dg), F32)
    for h in range(MEM_HEADS):
        in_head = (lane_q >= h * MEM_HEAD_DIM) & (lane_q < (h + 1) * MEM_HEAD_DIM)
        qh = jnp.where(in_head, qs, 0.0).astype(BF16)
        sc = lax.dot_general(qh, mk, (((1,), (1,)), ((), ())), preferred_element_type=F32)
        e = jnp.exp(sc - jnp.max(sc, axis=-1, keepdims=True))
        den = jnp.sum(e, axis=-1, keepdims=True)
        oh = jnp.dot(e.astype(BF16), mv, preferred_element_type=F32) / den
        y_mem = jnp.where(in_head, oh, y_mem)

    gng = gng_ref[...]
    groups = []
    for gi, y in enumerate((y_conv, y_pool, y_gmlp, y_mem)):
        ms = jnp.mean(y * y, axis=-1, keepdims=True)
        groups.append(y * lax.rsqrt(ms + RMS_EPS) * gng[:, gi * dg:(gi + 1) * dg])
    cat = jnp.concatenate(groups, axis=1).astype(BF16)
    mix = jnp.dot(cat, wout_ref[...], preferred_element_type=F32)

    h1 = _layer_norm(DEEPNORM_ALPHA * x + mix, ln1g_ref[...], ln1b_ref[...])
    for j in range(ROW_TILE_ROWS):
        h1rt_ref[pl.ds(j, ts, stride=ROW_TILE_ROWS), :] = h1[:, j * LANES:(j + 1) * LANES]

    h1_hi = h1.astype(BF16)
    h1_lo = (h1 - h1_hi.astype(F32)).astype(BF16)
    logits = (jnp.dot(h1_hi, wrh_ref[...], preferred_element_type=F32)
              + jnp.dot(h1_lo, wrh_ref[...], preferred_element_type=F32)
              + jnp.dot(h1_hi, wrl_ref[...], preferred_element_type=F32)) + br_ref[...]
    lane = lax.broadcasted_iota(jnp.int32, (ts, LANES), 1)
    lane_f = lane.astype(F32)
    vals, idxs = [], []
    cur_l = logits
    for _ in range(TOP_K):
        m = jnp.max(cur_l, axis=-1, keepdims=True)
        i_f = jnp.min(jnp.where(cur_l == m, lane_f, float(LANES)), axis=-1, keepdims=True)
        vals.append(m)
        idxs.append(i_f)
        cur_l = jnp.where(lane_f == i_f, -jnp.inf, cur_l)
    es = [jnp.exp(vk - vals[0]) for vk in vals]
    den = es[0] + es[1] + es[2] + es[3]
    idx_out = jnp.zeros((ts, LANES), F32)
    gate_out = jnp.zeros((ts, LANES), F32)
    hits = jnp.zeros((ts, LANES), F32)
    for k in range(TOP_K):
        idx_out = jnp.where(lane == k, idxs[k], idx_out)
        gate_out = jnp.where(lane == k, es[k] / den, gate_out)
        hits = hits + jnp.where(lane_f == idxs[k], 1.0, 0.0)
    tok = (pl.program_id(0) * pl.num_programs(1) + s) * ts + lax.broadcasted_iota(jnp.int32, (ts, LANES), 0)
    group = (tok >> block_shift) * GROUP_SLOTS + idx_out.astype(jnp.int32)
    key_ref[...] = jnp.where(lane < TOP_K, group * (1 << key_shift) + tok * TOP_K + lane, 0)
    gate_ref[...] = gate_out
    cnt_ref[0] = jnp.sum(hits, axis=0, keepdims=True)


def _mixer(h, kv_all, layer, lw, *, bsz, seq, ts, key_shift, block_shift):
    n = bsz * seq
    st = seq // ts
    dg = D_GROUP
    const = lambda b, s: (0, 0)
    kern = functools.partial(_mixer_kernel, ts=ts, key_shift=key_shift, block_shift=block_shift)
    return pl.pallas_call(
        kern,
        grid=(bsz, st),
        in_specs=[
            pl.BlockSpec((ts * ROW_TILE_ROWS, LANES), lambda b, s: (b * st + s, 0)),
            pl.BlockSpec((1, kv_all.shape[1], 2 * dg), lambda b, s: (b, 0, layer)),
            pl.BlockSpec((D_MODEL, D_IN_PROJ), const),
            pl.BlockSpec((3, dg), const),
            pl.BlockSpec((dg, dg), const),
            pl.BlockSpec((1, dg), const),
            pl.BlockSpec((1, dg), const),
            pl.BlockSpec((1, dg), const),
            pl.BlockSpec((GMLP_HEADS * GMLP_BLOCK, GMLP_BLOCK), const),
            pl.BlockSpec((GMLP_BLOCK, dg), const),
            pl.BlockSpec((1, D_MODEL), const),
            pl.BlockSpec((D_MODEL, D_MODEL), const),
            pl.BlockSpec((1, D_MODEL), const),
            pl.BlockSpec((1, D_MODEL), const),
            pl.BlockSpec((D_MODEL, LANES), const),
            pl.BlockSpec((D_MODEL, LANES), const),
            pl.BlockSpec((1, LANES), const),
        ],
        out_specs=[
            pl.BlockSpec((ts * ROW_TILE_ROWS, LANES), lambda b, s: (b * st + s, 0)),
            pl.BlockSpec((ts, LANES), lambda b, s: (b * st + s, 0)),
            pl.BlockSpec((ts, LANES), lambda b, s: (b * st + s, 0)),
            pl.BlockSpec((1, 1, LANES), lambda b, s: (b * st + s, 0, 0)),
        ],
        out_shape=[
            jax.ShapeDtypeStruct((n * ROW_TILE_ROWS, LANES), F32),
            jax.ShapeDtypeStruct((n, LANES), jnp.int32),
            jax.ShapeDtypeStruct((n, LANES), F32),
            jax.ShapeDtypeStruct((bsz * st, 1, LANES), F32),
        ],
        scratch_shapes=[pltpu.VMEM((ts + HALO, 2 * dg), F32)],
        compiler_params=pltpu.CompilerParams(
            dimension_semantics=("arbitrary", "arbitrary"), vmem_limit_bytes=VMEM_LIMIT_BYTES),
        name="mixer",
    )(h, kv_all, *lw)


def _expert_kernel(te_ref, nact_ref, winfo_ref, slot_ref, gate_ref,
                   h_hbm, w1_hbm, b1_ref, w2_hbm, b2_ref, g_ref, bt_ref, o_hbm,
                   xblk, ybuf, sin, sout, hbuf, w1buf, w2buf, fstage, xsem, osem, wsem, fsem,
                   *, tm, tbt, tpb, layer, tiled_out):
    i = pl.program_id(0)
    b = i // tpb
    tj = i - b * tpb
    n_act = nact_ref[b]
    par = lax.rem(i, 2)
    rt = ROW_TILE_ROWS
    blk_rows = tbt * rt
    sp = STAGE_PITCH
    cw = D_FF // EXPERT_CHUNKS
    base = (tj + 1) * tm

    def zero_row(tiles, width):
        bits = pltpu.bitcast(tiles[0], jnp.uint32)
        for t in tiles[1:]:
            bits = bits | pltpu.bitcast(t, jnp.uint32)
        z = ((bits >> 16) >> 16).astype(F32)[0:1, :]
        return jnp.concatenate([z] * (width // LANES), axis=1)

    def gather(first_entry, dst, lo=0, hi=tm):
        tiles = []
        entries = slot_ref.at[0, 0, pl.ds(first_entry, tm)]
        for r in range(lo, hi):
            off = pl.multiple_of((entries[r] >> 2) * rt, rt)
            tiles.append(xblk[pl.ds(off, rt), :])
            dst[pl.ds(r * rt, rt), :] = tiles[-1]
        return tiles

    def accumulate(first_entry, src, lo=0, hi=tm):
        sums = []
        entries = slot_ref.at[0, 0, pl.ds(first_entry, tm)]
        for r0 in range(lo, hi, ROW_BATCH):
            rs = range(r0, r0 + ROW_BATCH)
            slots = [entries[r] for r in rs]
            offs = [pl.multiple_of((s >> 2) * rt, rt) for s in slots]
            accs = [ybuf[pl.ds(o, rt), :] for o in offs]
            vals = [src[pl.ds(r, rt, stride=sp), :] for r in rs]
            for k, r in enumerate(rs):
                gate = gate_ref[0, 0, slots[k]]
                sums.append(accs[k] + gate * vals[k])
                ybuf[pl.ds(offs[k], rt), :] = sums[-1]
        return sums

    @pl.when(i == 0)
    def _():
        xblk[pl.ds(blk_rows, rt), :] = jnp.zeros((rt, LANES), F32)
        sout[...] = jnp.zeros(sout.shape, F32)

    def weight_copies(e, wslot):
        cps = []
        for src, dst in ((w1_hbm, w1buf), (w2_hbm, w2buf)):
            rows_p = src.shape[2] // WEIGHT_DMA_PARTS
            for p in range(WEIGHT_DMA_PARTS):
                sl = pl.ds(p * rows_p, rows_p)
                cps.append((pltpu.make_async_copy(src.at[layer, e, sl], dst.at[wslot, sl], wsem.at[wslot]), p % 2))
        return cps

    @pl.when(tj < n_act)
    def _():
        first = tj == 0
        last = tj == n_act - 1
        row0 = pl.multiple_of(b * blk_rows, blk_rows)
        info = winfo_ref[i]
        wpar = (info >> 2) & 1

        @pl.when(i == 0)
        def _():
            for cp, prio in weight_copies(te_ref[0], 0):
                cp.start(priority=prio)

        @pl.when((info & 1) == 1)
        def _():
            for cp, _ in weight_copies(te_ref[i], wpar):
                cp.wait()

            @pl.when((info & 2) == 2)
            def _():
                for cp, prio in weight_copies(info >> 3, 1 - wpar):
                    cp.start(priority=prio)

        w1_ref = w1buf.at[wpar]
        w2_ref = w2buf.at[wpar]

        @pl.when(first)
        def _():
            load = pltpu.make_async_copy(h_hbm.at[pl.ds(row0, blk_rows), :], xblk.at[pl.ds(0, blk_rows), :], xsem)
            load.start()
            zrows = FINAL_CHUNK * rt

            def zero(c, carry):
                ybuf[pl.ds(pl.multiple_of(c * zrows, zrows), zrows), :] = jnp.zeros((zrows, LANES), F32)
                return carry

            lax.fori_loop(0, tbt // FINAL_CHUNK, zero, 0)
            ybuf[pl.ds(blk_rows, rt), :] = jnp.zeros((rt, LANES), F32)
            load.wait()
            gather(base, sin.at[par])

        xs = sin.at[par]
        x = jnp.concatenate([xs[pl.ds(j, tm, stride=rt), :] for j in range(rt)], axis=1)
        xb = x.astype(BF16)

        per1 = tm // (2 * EXPERT_CHUNKS)
        per2 = tm // EXPERT_CHUNKS
        for c in range(2 * EXPERT_CHUNKS):
            tie = zero_row(accumulate(base - tm, sout.at[1 - par], c * per1, (c + 1) * per1), cw)
            hbuf[:, c * cw:(c + 1) * cw] = (jnp.dot(xb, w1_ref[:, c * cw:(c + 1) * cw], preferred_element_type=F32)
                                            + (b1_ref[:, c * cw:(c + 1) * cw] + tie))
        out = None
        for c in range(EXPERT_CHUNKS):
            tie = zero_row(gather(base + tm, sin.at[1 - par], c * per2, (c + 1) * per2), cw)
            glu = jnp.minimum(hbuf[:, c * cw:(c + 1) * cw], SWIGLU_LIMIT)
            lin = jnp.clip(hbuf[:, D_FF + c * cw:D_FF + (c + 1) * cw] + tie, -SWIGLU_LIMIT, SWIGLU_LIMIT)
            act = glu * jax.nn.sigmoid(SWIGLU_ALPHA * glu) * (lin + 1.0)
            part = jnp.dot(act.astype(BF16), w2_ref[c * cw:(c + 1) * cw, :], preferred_element_type=F32)
            out = part if out is None else out + part
        out = out + b2_ref[...]
        so = sout.at[par]
        for j in range(rt):
            so[pl.ds(j * sp, tm), :] = out[:, j * LANES:(j + 1) * LANES]

        @pl.when(last)
        def _():
            accumulate(base, so)
            crows = FINAL_CHUNK * rt

            def finish(c, carry):
                c0 = pl.multiple_of(c * crows, crows)
                r = jnp.concatenate(
                    [DEEPNORM_ALPHA * xblk[pl.ds(c0 + jj, FINAL_CHUNK, stride=rt), :]
                     + ybuf[pl.ds(c0 + jj, FINAL_CHUNK, stride=rt), :] for jj in range(rt)], axis=1)
                y = _layer_norm(r, g_ref[...], bt_ref[...])
                if tiled_out:
                    for jj in range(rt):
                        ybuf[pl.ds(c0 + jj, FINAL_CHUNK, stride=rt), :] = y[:, jj * LANES:(jj + 1) * LANES]
                    pltpu.make_async_copy(
                        ybuf.at[pl.ds(c0, crows), :],
                        o_hbm.at[pl.ds(pl.multiple_of(row0 + c * crows, crows), crows), :], osem).start()
                else:
                    p = lax.rem(c, 2)

                    @pl.when(c >= 2)
                    def _():
                        pltpu.make_async_copy(fstage.at[p], o_hbm.at[pl.ds(0, FINAL_CHUNK), :], fsem.at[p]).wait()

                    fstage[p] = y
                    tok0 = pl.multiple_of(b * tbt + c * FINAL_CHUNK, FINAL_CHUNK)
                    pltpu.make_async_copy(fstage.at[p], o_hbm.at[pl.ds(tok0, FINAL_CHUNK), :], fsem.at[p]).start()
                return carry

            lax.fori_loop(0, tbt // FINAL_CHUNK, finish, 0)
            if tiled_out:
                pltpu.make_async_copy(ybuf.at[pl.ds(0, blk_rows), :], o_hbm.at[pl.ds(row0, blk_rows), :], osem).wait()
            else:
                for p in range(2):
                    pltpu.make_async_copy(fstage.at[p], o_hbm.at[pl.ds(0, FINAL_CHUNK), :], fsem.at[p]).wait()


def _experts(h1rt, te, nact, winfo, slots_blk, gates_blk, w1b, b1, w2b, b2, ln_g, ln_b, layer, *, tm, tbt, tiled_out):
    nt = te.shape[0]
    tpb = nt // nact.shape[0]
    n_rows = h1rt.shape[0]
    out_shape = (n_rows, LANES) if tiled_out else (n_rows // ROW_TILE_ROWS, D_MODEL)
    kern = functools.partial(_expert_kernel, tm=tm, tbt=tbt, tpb=tpb, layer=layer, tiled_out=tiled_out)
    per_block = lambda i, te, na, wi: (i // tpb, 0, 0)
    per_expert = lambda i, te, na, wi: (layer, te[i], 0, 0)
    const = lambda i, te, na, wi: (0, 0)
    grid_spec = pltpu.PrefetchScalarGridSpec(
        num_scalar_prefetch=3,
        grid=(nt,),
        in_specs=[
            pl.BlockSpec((1, 1, slots_blk.shape[2]), per_block, memory_space=pltpu.SMEM,
                         pipeline_mode=pl.Buffered(1)),
            pl.BlockSpec((1, 1, gates_blk.shape[2]), per_block, memory_space=pltpu.SMEM,
                         pipeline_mode=pl.Buffered(1)),
            pl.BlockSpec(memory_space=pl.ANY),
            pl.BlockSpec(memory_space=pl.ANY),
            pl.BlockSpec((None, None, 1, 2 * D_FF), per_expert),
            pl.BlockSpec(memory_space=pl.ANY),
            pl.BlockSpec((None, None, 1, D_MODEL), per_expert),
            pl.BlockSpec((1, D_MODEL), const),
            pl.BlockSpec((1, D_MODEL), const),
        ],
        out_specs=pl.BlockSpec(memory_space=pl.ANY),
        scratch_shapes=[
            pltpu.VMEM(((tbt + 1) * ROW_TILE_ROWS, LANES), F32),
            pltpu.VMEM(((tbt + 1) * ROW_TILE_ROWS, LANES), F32),
            pltpu.VMEM((2, tm * ROW_TILE_ROWS, LANES), F32),
            pltpu.VMEM((2, ROW_TILE_ROWS * STAGE_PITCH, LANES), F32),
            pltpu.VMEM((tm, 2 * D_FF), F32),
            pltpu.VMEM((2, D_MODEL, 2 * D_FF), BF16),
            pltpu.VMEM((2, D_FF, D_MODEL), BF16),
            pltpu.VMEM((2, FINAL_CHUNK, D_MODEL), F32),
            pltpu.SemaphoreType.DMA,
            pltpu.SemaphoreType.DMA,
            pltpu.SemaphoreType.DMA((2,)),
            pltpu.SemaphoreType.DMA((2,)),
        ],
    )
    return pl.pallas_call(
        kern,
        grid_spec=grid_spec,
        out_shape=jax.ShapeDtypeStruct(out_shape, F32),
        compiler_params=pltpu.CompilerParams(
            dimension_semantics=("arbitrary",), vmem_limit_bytes=VMEM_LIMIT_BYTES),
        name="experts",
    )(te, nact, winfo, slots_blk, gates_blk, h1rt, w1b, b1, w2b, b2, ln_g, ln_b)


def _key_shift(n, tm):
    return max((n * TOP_K - 1).bit_length(), (N_EXPERTS * tm - 1).bit_length()) + 1


def _routing_plan(keys, counts, n, tm, tbt):
    nb = n // tbt
    tpb = tbt * TOP_K // tm + N_EXPERTS
    shift = _key_shift(n, tm)
    flag = 1 << (shift - 1)
    need = (-counts) % tm
    b_ids = jnp.arange(nb, dtype=jnp.int32)[:, None, None]
    e_ids = jnp.arange(N_EXPERTS, dtype=jnp.int32)[None, :, None]
    j_ids = jnp.arange(tm, dtype=jnp.int32)[None, None, :]
    pad_keys = jnp.where(j_ids < need[:, :, None],
                         (b_ids * GROUP_SLOTS + e_ids) * (1 << shift) + flag + j_ids,
                         (b_ids * GROUP_SLOTS + N_EXPERTS) * (1 << shift) + flag + e_ids * tm + j_ids)
    ks = jnp.sort(jnp.concatenate([keys, pad_keys.reshape(-1)])).reshape(nb, tpb * tm)
    low = ks & (flag - 1)
    real = (ks & flag) == 0
    slot = jnp.where(real, low & (tbt * TOP_K - 1), tbt * TOP_K)
    edge = jnp.full((nb, tm), tbt * TOP_K, jnp.int32)
    slots_blk = jnp.concatenate([edge, slot, edge], axis=1).reshape(nb, 1, (tpb + 2) * tm)
    e_t = (ks[:, ::tm] >> shift) % GROUP_SLOTS
    active = e_t < N_EXPERTS
    nact = jnp.sum(active.astype(jnp.int32), axis=1)
    active = active.reshape(-1)
    te = jnp.where(active, e_t.reshape(-1), N_EXPERTS - 1)
    nt = nb * tpb
    tile = jnp.arange(nt, dtype=jnp.int32)
    prev_same = (tile % tpb != 0) & (jnp.roll(te, 1) == te) & jnp.roll(active, 1)
    gstart = active & ~prev_same
    gidx = jnp.cumsum(gstart.astype(jnp.int32)) - 1
    starts = jnp.where(gstart, tile, nt)
    nxt = jnp.flip(lax.cummin(jnp.flip(jnp.roll(starts, -1).at[-1].set(nt))))
    has_next = nxt < nt
    tnext = te[jnp.minimum(nxt, nt - 1)]
    winfo = (gstart.astype(jnp.int32) | (has_next.astype(jnp.int32) << 1) | ((gidx & 1) << 2) | (tnext << 3))
    return te, nact, winfo, slots_blk


def _pick_tile(total, want):
    t = min(total, want)
    while total % t:
        t //= 2
    return t


def kernel(x, mem, ln_in_g, ln_in_b, mem_ln_g, mem_ln_b, w_in, conv_w, pool_w, pool_scale, gmlp_ln_g, gmlp_ln_b,
           gmlp_w, gmlp_b, w_mem_kv, group_norm_g, w_out, ln1_g, ln1_b, w_router, b_router, w1, b1, w2, b2,
           ln2_g, ln2_b):
    bsz, seq, d = x.shape
    depth = w_in.shape[0]
    n = bsz * seq
    dg = D_GROUP
    ts = _pick_tile(seq, 512)
    tm = EXPERT_TILE
    tbt = _pick_tile(n, EXPERT_BLOCK_TOKENS)
    nb = n // tbt
    assert N_EXPERTS == 32 and tbt % ts == 0 and tbt & (tbt - 1) == 0 and tbt % FINAL_CHUNK == 0
    key_shift = _key_shift(n, tm)
    block_shift = tbt.bit_length() - 1

    h = _ln_rows(x.reshape(n, d), ln_in_g, ln_in_b, _pick_tile(n, 1024))

    wkv_all = jnp.transpose(w_mem_kv, (1, 0, 2)).reshape(d, depth * 2 * dg).astype(BF16)
    kv_all = _mem_kv(mem, mem_ln_g, mem_ln_b, wkv_all)

    pos = jnp.arange(GMLP_BLOCK)
    mask = (pos[None, :] // CHUNK) <= (pos[:, None] // CHUNK)
    eye = jnp.eye(len(POOL_WINDOWS), dtype=F32)

    wr_pad = jnp.zeros((depth, d, LANES), F32).at[:, :, :N_EXPERTS].set(w_router)
    wr_hi = wr_pad.astype(BF16)
    wr_lo = (wr_pad - wr_hi.astype(F32)).astype(BF16)
    br_pad = jnp.full((depth, 1, LANES), NEG_BIG, F32).at[:, 0, :N_EXPERTS].set(b_router)
    w1b = w1.astype(BF16)
    w2b = w2.astype(BF16)
    b1r = b1.reshape(depth, N_EXPERTS, 1, 2 * D_FF)
    b2r = b2.reshape(depth, N_EXPERTS, 1, d)

    for l in range(depth):
        pool_bd = jnp.einsum('gh,gcd->gchd', eye, pool_w[l]).reshape(dg, dg).astype(BF16)
        gw = jnp.where(mask[None], gmlp_w[l], 0.0).reshape(GMLP_HEADS * GMLP_BLOCK, GMLP_BLOCK).astype(BF16)
        gbias = jnp.repeat(gmlp_b[l].T, GMLP_HEAD_DIM, axis=1)
        lw = (w_in[l].astype(BF16), conv_w[l], pool_bd, pool_scale[l].reshape(1, dg),
              gmlp_ln_g[l].reshape(1, dg), gmlp_ln_b[l].reshape(1, dg), gw, gbias,
              group_norm_g[l].reshape(1, d), w_out[l].astype(BF16),
              ln1_g[l].reshape(1, d), ln1_b[l].reshape(1, d), wr_hi[l], wr_lo[l], br_pad[l])
        h1rt, keys, gates, cnt = _mixer(h, kv_all, l, lw, bsz=bsz, seq=seq, ts=ts,
                                        key_shift=key_shift, block_shift=block_shift)
        counts = jnp.sum(cnt.reshape(nb, tbt // ts, LANES), axis=1)[:, :N_EXPERTS].astype(jnp.int32)
        te, nact, winfo, slots_blk = _routing_plan(keys[:, :TOP_K].reshape(-1), counts, n, tm, tbt)
        gates_blk = jnp.pad(gates[:, :TOP_K].reshape(nb, 1, tbt * TOP_K), ((0, 0), (0, 0), (0, LANES)))
        h = _experts(h1rt, te, nact, winfo, slots_blk, gates_blk, w1b, b1r, w2b, b2r,
                     ln2_g[l].reshape(1, d), ln2_b[l].reshape(1, d), l,
                     tm=tm, tbt=tbt, tiled_out=l + 1 < depth)
    return h.reshape(bsz, seq, d)
```

```python
import functools

import jax
import jax.numpy as jnp
from jax import lax
from jax.experimental import pallas as pl
from jax.experimental.pallas import tpu as pltpu

F32 = jnp.float32
BF16 = jnp.bfloat16

D_MODEL = 1024
N_GROUPS = 4
D_GROUP = D_MODEL // N_GROUPS
D_IN_PROJ = 7 * D_GROUP
POOL_WINDOWS = (2, 4, 8, 16)
POOL_CH = D_GROUP // len(POOL_WINDOWS)
GMLP_BLOCK = 128
GMLP_HEADS = 4
GMLP_HEAD_DIM = D_GROUP // GMLP_HEADS
CHUNK = 64
MEM_HEADS = 4
MEM_HEAD_DIM = D_GROUP // MEM_HEADS
N_EXPERTS = 32
GROUP_SLOTS = N_EXPERTS + 1
TOP_K = 4
D_FF = D_MODEL
SWIGLU_ALPHA = 1.702
SWIGLU_LIMIT = 7.0
REF_DEPTH = 4
DEEPNORM_ALPHA = (2 * REF_DEPTH) ** 0.25
LN_EPS = 1e-5
RMS_EPS = 1e-6

SUBLANES = 8
LANES = 128
ROW_TILE_ROWS = D_MODEL // LANES
HALO = max(POOL_WINDOWS)
VMEM_LIMIT_BYTES = 58 * 1024 * 1024
EXPERT_CHUNKS = 4
EXPERT_TILE = 256
EXPERT_BLOCK_TOKENS = 4096
STAGE_PITCH = EXPERT_TILE + SUBLANES
FINAL_CHUNK = 128
ROW_BATCH = 8
WEIGHT_DMA_PARTS = 4

NEG_BIG = -1e30


def _layer_norm(x, g, b):
    mu = jnp.mean(x, axis=-1, keepdims=True)
    xc = x - mu
    var = jnp.mean(xc * xc, axis=-1, keepdims=True)
    return xc * lax.rsqrt(var + LN_EPS) * g + b


def _ln_kernel(x_ref, g_ref, b_ref, o_ref, *, rows):
    y = _layer_norm(x_ref[...], g_ref[...], b_ref[...])
    for j in range(ROW_TILE_ROWS):
        o_ref[pl.ds(j, rows, stride=ROW_TILE_ROWS), :] = y[:, j * LANES:(j + 1) * LANES]


def _ln_rows(x, g, b, rows):
    n, d = x.shape
    return pl.pallas_call(
        functools.partial(_ln_kernel, rows=rows),
        grid=(n // rows,),
        in_specs=[pl.BlockSpec((rows, d), lambda i: (i, 0)),
                  pl.BlockSpec((1, d), lambda i: (0, 0)),
                  pl.BlockSpec((1, d), lambda i: (0, 0))],
        out_specs=pl.BlockSpec((rows * ROW_TILE_ROWS, LANES), lambda i: (i, 0)),
        out_shape=jax.ShapeDtypeStruct((n * ROW_TILE_ROWS, LANES), F32),
        name="entry_ln",
    )(x, g.reshape(1, d), b.reshape(1, d))


def _mem_kv_kernel(m_ref, g_ref, b_ref, w_ref, o_ref):
    mn = _layer_norm(m_ref[0], g_ref[...], b_ref[...])
    o_ref[0] = jnp.dot(mn.astype(BF16), w_ref[...], preferred_element_type=F32).astype(BF16)


def _mem_kv(mem, g, b, w_all):
    bsz, m, d = mem.shape
    p = w_all.shape[1]
    return pl.pallas_call(
        _mem_kv_kernel,
        grid=(bsz,),
        in_specs=[pl.BlockSpec((1, m, d), lambda i: (i, 0, 0)),
                  pl.BlockSpec((1, d), lambda i: (0, 0)),
                  pl.BlockSpec((1, d), lambda i: (0, 0)),
                  pl.BlockSpec((d, p), lambda i: (0, 0))],
        out_specs=pl.BlockSpec((1, m, p), lambda i: (i, 0, 0)),
        out_shape=jax.ShapeDtypeStruct((bsz, m, p), BF16),
        name="mem_kv",
    )(mem, g.reshape(1, d), b.reshape(1, d), w_all)


def _mixer_kernel(x_ref, kv_ref, win_ref, convw_ref, poolw_ref, pscale_ref, glng_ref, glnb_ref,
                  gw_ref, gbias_ref, gng_ref, wout_ref, ln1g_ref, ln1b_ref, wrh_ref, wrl_ref, br_ref,
                  h1rt_ref, key_ref, gate_ref, cnt_ref, ext_ref, *, ts, key_shift, block_shift):
    s = pl.program_id(1)
    dg = D_GROUP

    x = jnp.concatenate([x_ref[pl.ds(j, ts, stride=ROW_TILE_ROWS), :] for j in range(ROW_TILE_ROWS)], axis=1)
    proj = jnp.dot(x.astype(BF16), win_ref[...], preferred_element_type=F32)
    gate_b = proj[:, 0:dg]
    gate_c = proj[:, dg:2 * dg]
    conv_in = proj[:, 2 * dg:3 * dg]
    pool_in = proj[:, 3 * dg:4 * dg]
    gmlp_in = proj[:, 4 * dg:6 * dg]
    mem_q = proj[:, 6 * dg:7 * dg]

    @pl.when(s == 0)
    def _():
        ext_ref[0:HALO, :] = jnp.zeros((HALO, 2 * dg), F32)

    z = gate_c * conv_in
    ext_ref[HALO:HALO + ts, 0:dg] = z
    ext_ref[HALO:HALO + ts, dg:2 * dg] = pool_in

    zm1 = ext_ref[pl.ds(HALO - 1, ts), 0:dg]
    zm2 = ext_ref[pl.ds(HALO - 2, ts), 0:dg]
    cw = convw_ref[...]
    y_conv = gate_b * (cw[0:1, :] * zm2 + cw[1:2, :] * zm1 + cw[2:3, :] * z)

    pos = s * ts + lax.broadcasted_iota(jnp.int32, (ts, 1), 0)
    half = dg // 2
    lane_h = lax.broadcasted_iota(jnp.int32, (ts, half), 1)
    diffs = []
    for hf in range(2):
        c0 = dg + hf * half
        w_lo, w_hi = POOL_WINDOWS[2 * hf], POOL_WINDOWS[2 * hf + 1]
        cur = ext_ref[pl.ds(HALO, ts), c0:c0 + half]
        acc = cur
        for k in range(1, w_lo):
            acc = acc + ext_ref[pl.ds(HALO - k, ts), c0:c0 + half]
        s_lo = acc
        for k in range(w_lo, w_hi):
            acc = acc + ext_ref[pl.ds(HALO - k, ts), c0:c0 + half]
        s_hi = acc
        cnt_lo = jnp.minimum(pos + 1, w_lo).astype(F32)
        cnt_hi = jnp.minimum(pos + 1, w_hi).astype(F32)
        mean = jnp.where(lane_h < POOL_CH, s_lo / cnt_lo, s_hi / cnt_hi)
        diffs.append(mean - cur)
    dpool = jnp.concatenate(diffs, axis=1)
    y_pool = jnp.dot(dpool.astype(BF16), poolw_ref[...], preferred_element_type=F32) * pscale_ref[...]

    ext_ref[0:HALO, :] = ext_ref[ts:ts + HALO, :]

    zg = jax.nn.gelu(gmlp_in, approximate=True)
    u = zg[:, 0:dg]
    v = _layer_norm(zg[:, dg:2 * dg], glng_ref[...], glnb_ref[...]).astype(BF16)
    lane_g = lax.broadcasted_iota(jnp.int32, (GMLP_BLOCK, dg), 1)
    gw = gw_ref[...]
    gbias = gbias_ref[...]
    gates = []
    for nb in range(ts // GMLP_BLOCK):
        vb = v[nb * GMLP_BLOCK:(nb + 1) * GMLP_BLOCK, :]
        r = jnp.dot(gw, vb, preferred_element_type=F32)
        g = r[0:GMLP_BLOCK, :]
        for h in range(1, GMLP_HEADS):
            g = jnp.where(lane_g >= h * GMLP_HEAD_DIM, r[h * GMLP_BLOCK:(h + 1) * GMLP_BLOCK, :], g)
        gates.append(g + gbias)
    y_gmlp = u * jnp.concatenate(gates, axis=0)

    kv = kv_ref[0]
    mk = kv[:, 0:dg]
    mv = kv[:, dg:2 * dg]
    lane_q = lax.broadcasted_iota(jnp.int32, (ts, dg), 1)
    qs = mem_q * (MEM_HEAD_DIM ** -0.5)
    y_mem = jnp.zeros((ts, dg), F32)
    for h in range(MEM_HEADS):
        in_head = (lane_q >= h * MEM_HEAD_DIM) & (lane_q < (h + 1) * MEM_HEAD_DIM)
        qh = jnp.where(in_head, qs, 0.0).astype(BF16)
        sc = lax.dot_general(qh, mk, (((1,), (1,)), ((), ())), preferred_element_type=F32)
        e = jnp.exp(sc - jnp.max(sc, axis=-1, keepdims=True))
        den = jnp.sum(e, axis=-1, keepdims=True)
        oh = jnp.dot(e.astype(BF16), mv, preferred_element_type=F32) / den
        y_mem = jnp.where(in_head, oh, y_mem)

    gng = gng_ref[...]
    groups = []
    for gi, y in enumerate((y_conv, y_pool, y_gmlp, y_mem)):
        ms = jnp.mean(y * y, axis=-1, keepdims=True)
        groups.append(y * lax.rsqrt(ms + RMS_EPS) * gng[:, gi * dg:(gi + 1) * dg])
    cat = jnp.concatenate(groups, axis=1).astype(BF16)
    mix = jnp.dot(cat, wout_ref[...], preferred_element_type=F32)

    h1 = _layer_norm(DEEPNORM_ALPHA * x + mix, ln1g_ref[...], ln1b_ref[...])
    for j in range(ROW_TILE_ROWS):
        h1rt_ref[pl.ds(j, ts, stride=ROW_TILE_ROWS), :] = h1[:, j * LANES:(j + 1) * LANES]

    h1_hi = h1.astype(BF16)
    h1_lo = (h1 - h1_hi.astype(F32)).astype(BF16)
    logits = (jnp.dot(h1_hi, wrh_ref[...], preferred_element_type=F32)
              + jnp.dot(h1_lo, wrh_ref[...], preferred_element_type=F32)
              + jnp.dot(h1_hi, wrl_ref[...], preferred_element_type=F32)) + br_ref[...]
    lane = lax.broadcasted_iota(jnp.int32, (ts, LANES), 1)
    lane_f = lane.astype(F32)
    vals, idxs = [], []
    cur_l = logits
    for _ in range(TOP_K):
        m = jnp.max(cur_l, axis=-1, keepdims=True)
        i_f = jnp.min(jnp.where(cur_l == m, lane_f, float(LANES)), axis=-1, keepdims=True)
        vals.append(m)
        idxs.append(i_f)
        cur_l = jnp.where(lane_f == i_f, -jnp.inf, cur_l)
    es = [jnp.exp(vk - vals[0]) for vk in vals]
    den = es[0] + es[1] + es[2] + es[3]
    idx_out = jnp.zeros((ts, LANES), F32)
    gate_out = jnp.zeros((ts, LANES), F32)
    hits = jnp.zeros((ts, LANES), F32)
    for k in range(TOP_K):
        idx_out = jnp.where(lane == k, idxs[k], idx_out)
        gate_out = jnp.where(lane == k, es[k] / den, gate_out)
        hits = hits + jnp.where(lane_f == idxs[k], 1.0, 0.0)
    tok = (pl.program_id(0) * pl.num_programs(1) + s) * ts + lax.broadcasted_iota(jnp.int32, (ts, LANES), 0)
    group = (tok >> block_shift) * GROUP_SLOTS + idx_out.astype(jnp.int32)
    key_ref[...] = jnp.where(lane < TOP_K, group * (1 << key_shift) + tok * TOP_K + lane, 0)
    gate_ref[...] = gate_out
    cnt_ref[0] = jnp.sum(hits, axis=0, keepdims=True)


def _mixer(h, kv_all, layer, lw, *, bsz, seq, ts, key_shift, block_shift):
    n = bsz * seq
    st = seq // ts
    dg = D_GROUP
    const = lambda b, s: (0, 0)
    kern = functools.partial(_mixer_kernel, ts=ts, key_shift=key_shift, block_shift=block_shift)
    return pl.pallas_call(
        kern,
        grid=(bsz, st),
        in_specs=[
            pl.BlockSpec((ts * ROW_TILE_ROWS, LANES), lambda b, s: (b * st + s, 0)),
            pl.BlockSpec((1, kv_all.shape[1], 2 * dg), lambda b, s: (b, 0, layer)),
            pl.BlockSpec((D_MODEL, D_IN_PROJ), const),
            pl.BlockSpec((3, dg), const),
            pl.BlockSpec((dg, dg), const),
            pl.BlockSpec((1, dg), const),
            pl.BlockSpec((1, dg), const),
            pl.BlockSpec((1, dg), const),
            pl.BlockSpec((GMLP_HEADS * GMLP_BLOCK, GMLP_BLOCK), const),
            pl.BlockSpec((GMLP_BLOCK, dg), const),
            pl.BlockSpec((1, D_MODEL), const),
            pl.BlockSpec((D_MODEL, D_MODEL), const),
            pl.BlockSpec((1, D_MODEL), const),
            pl.BlockSpec((1, D_MODEL), const),
            pl.BlockSpec((D_MODEL, LANES), const),
            pl.BlockSpec((D_MODEL, LANES), const),
            pl.BlockSpec((1, LANES), const),
        ],
        out_specs=[
            pl.BlockSpec((ts * ROW_TILE_ROWS, LANES), lambda b, s: (b * st + s, 0)),
            pl.BlockSpec((ts, LANES), lambda b, s: (b * st + s, 0)),
            pl.BlockSpec((ts, LANES), lambda b, s: (b * st + s, 0)),
            pl.BlockSpec((1, 1, LANES), lambda b, s: (b * st + s, 0, 0)),
        ],
        out_shape=[
            jax.ShapeDtypeStruct((n * ROW_TILE_ROWS, LANES), F32),
            jax.ShapeDtypeStruct((n, LANES), jnp.int32),
            jax.ShapeDtypeStruct((n, LANES), F32),
            jax.ShapeDtypeStruct((bsz * st, 1, LANES), F32),
        ],
        scratch_shapes=[pltpu.VMEM((ts + HALO, 2 * dg), F32)],
        compiler_params=pltpu.CompilerParams(
            dimension_semantics=("arbitrary", "arbitrary"), vmem_limit_bytes=VMEM_LIMIT_BYTES),
        name="mixer",
    )(h, kv_all, *lw)


def _expert_kernel(te_ref, nact_ref, winfo_ref, slot_ref, gate_ref,
                   h_hbm, w1_hbm, b1_ref, w2_hbm, b2_ref, g_ref, bt_ref, o_hbm,
                   xblk, ybuf, sin, sout, hbuf, w1buf, w2buf, fstage, xsem, osem, wsem, fsem,
                   *, tm, tbt, tpb, layer, tiled_out):
    i = pl.program_id(0)
    b = i // tpb
    tj = i - b * tpb
    n_act = nact_ref[b]
    par = lax.rem(i, 2)
    rt = ROW_TILE_ROWS
    blk_rows = tbt * rt
    sp = STAGE_PITCH
    cw = D_FF // EXPERT_CHUNKS
    base = (tj + 1) * tm

    def zero_row(tiles, width):
        bits = pltpu.bitcast(tiles[0], jnp.uint32)
        for t in tiles[1:]:
            bits = bits | pltpu.bitcast(t, jnp.uint32)
        z = ((bits >> 16) >> 16).astype(F32)[0:1, :]
        return jnp.concatenate([z] * (width // LANES), axis=1)

    def gather(first_entry, dst, lo=0, hi=tm):
        tiles = []
        entries = slot_ref.at[0, 0, pl.ds(first_entry, tm)]
        for r in range(lo, hi):
            off = pl.multiple_of((entries[r] >> 2) * rt, rt)
            tiles.append(xblk[pl.ds(off, rt), :])
            dst[pl.ds(r * rt, rt), :] = tiles[-1]
        return tiles

    def accumulate(first_entry, src, lo=0, hi=tm):
        sums = []
        entries = slot_ref.at[0, 0, pl.ds(first_entry, tm)]
        for r0 in range(lo, hi, ROW_BATCH):
            rs = range(r0, r0 + ROW_BATCH)
            slots = [entries[r] for r in rs]
            offs = [pl.multiple_of((s >> 2) * rt, rt) for s in slots]
            accs = [ybuf[pl.ds(o, rt), :] for o in offs]
            vals = [src[pl.ds(r, rt, stride=sp), :] for r in rs]
            for k, r in enumerate(rs):
                gate = gate_ref[0, 0, slots[k]]
                sums.append(accs[k] + gate * vals[k])
                ybuf[pl.ds(offs[k], rt), :] = sums[-1]
        return sums

    @pl.when(i == 0)
    def _():
        xblk[pl.ds(blk_rows, rt), :] = jnp.zeros((rt, LANES), F32)
        sout[...] = jnp.zeros(sout.shape, F32)

    def weight_copies(e, wslot):
        cps = []
        for src, dst in ((w1_hbm, w1buf), (w2_hbm, w2buf)):
            rows_p = src.shape[2] // WEIGHT_DMA_PARTS
            for p in range(WEIGHT_DMA_PARTS):
                sl = pl.ds(p * rows_p, rows_p)
                cps.append((pltpu.make_async_copy(src.at[layer, e, sl], dst.at[wslot, sl], wsem.at[wslot]), p % 2))
        return cps

    @pl.when(tj < n_act)
    def _():
        first = tj == 0
        last = tj == n_act - 1
        row0 = pl.multiple_of(b * blk_rows, blk_rows)
        info = winfo_ref[i]
        wpar = (info >> 2) & 1

        @pl.when(i == 0)
        def _():
            for cp, prio in weight_copies(te_ref[0], 0):
                cp.start(priority=prio)

        @pl.when((info & 1) == 1)
        def _():
            for cp, _ in weight_copies(te_ref[i], wpar):
                cp.wait()

            @pl.when((info & 2) == 2)
            def _():
                for cp, prio in weight_copies(info >> 4, 1 - wpar):
                    cp.start(priority=prio)

        w1_ref = w1buf.at[wpar]
        w2_ref = w2buf.at[wpar]

        @pl.when(first)
        def _():
            load = pltpu.make_async_copy(h_hbm.at[pl.ds(row0, blk_rows), :], xblk.at[pl.ds(0, blk_rows), :], xsem)
            load.start()
            zrows = FINAL_CHUNK * rt

            def zero(c, carry):
                ybuf[pl.ds(pl.multiple_of(c * zrows, zrows), zrows), :] = jnp.zeros((zrows, LANES), F32)
                return carry

            lax.fori_loop(0, tbt // FINAL_CHUNK, zero, 0)
            ybuf[pl.ds(blk_rows, rt), :] = jnp.zeros((rt, LANES), F32)
            load.wait()
            gather(base, sin.at[par])

        so = sout.at[par]

        def expert(m):
            xs = sin.at[par]
            x = jnp.concatenate([xs[pl.ds(j, m, stride=rt), :] for j in range(rt)], axis=1)
            xb = x.astype(BF16)
            per1 = tm // (2 * EXPERT_CHUNKS)
            per2 = tm // EXPERT_CHUNKS
            for c in range(2 * EXPERT_CHUNKS):
                tie = zero_row(accumulate(base - tm, sout.at[1 - par], c * per1, (c + 1) * per1), cw)
                hbuf[0:m, c * cw:(c + 1) * cw] = (
                    jnp.dot(xb, w1_ref[:, c * cw:(c + 1) * cw], preferred_element_type=F32)
                    + (b1_ref[:, c * cw:(c + 1) * cw] + tie))
            out = None
            for c in range(EXPERT_CHUNKS):
                tie = zero_row(gather(base + tm, sin.at[1 - par], c * per2, (c + 1) * per2), cw)
                glu = jnp.minimum(hbuf[0:m, c * cw:(c + 1) * cw], SWIGLU_LIMIT)
                lin = jnp.clip(hbuf[0:m, D_FF + c * cw:D_FF + (c + 1) * cw] + tie, -SWIGLU_LIMIT, SWIGLU_LIMIT)
                act = glu * jax.nn.sigmoid(SWIGLU_ALPHA * glu) * (lin + 1.0)
                part = jnp.dot(act.astype(BF16), w2_ref[c * cw:(c + 1) * cw, :], preferred_element_type=F32)
                out = part if out is None else out + part
            out = out + b2_ref[...]
            for j in range(rt):
                so[pl.ds(j * sp, m), :] = out[:, j * LANES:(j + 1) * LANES]

        half = (info & 8) == 8
        pl.when(half)(functools.partial(expert, tm // 2))
        pl.when(jnp.logical_not(half))(functools.partial(expert, tm))

        @pl.when(last)
        def _():
            accumulate(base, so)
            crows = FINAL_CHUNK * rt

            def finish(c, carry):
                c0 = pl.multiple_of(c * crows, crows)
                r = jnp.concatenate(
                    [DEEPNORM_ALPHA * xblk[pl.ds(c0 + jj, FINAL_CHUNK, stride=rt), :]
                     + ybuf[pl.ds(c0 + jj, FINAL_CHUNK, stride=rt), :] for jj in range(rt)], axis=1)
                y = _layer_norm(r, g_ref[...], bt_ref[...])
                if tiled_out:
                    for jj in range(rt):
                        ybuf[pl.ds(c0 + jj, FINAL_CHUNK, stride=rt), :] = y[:, jj * LANES:(jj + 1) * LANES]
                    pltpu.make_async_copy(
                        ybuf.at[pl.ds(c0, crows), :],
                        o_hbm.at[pl.ds(pl.multiple_of(row0 + c * crows, crows), crows), :], osem).start()
                else:
                    p = lax.rem(c, 2)

                    @pl.when(c >= 2)
                    def _():
                        pltpu.make_async_copy(fstage.at[p], o_hbm.at[pl.ds(0, FINAL_CHUNK), :], fsem.at[p]).wait()

                    fstage[p] = y
                    tok0 = pl.multiple_of(b * tbt + c * FINAL_CHUNK, FINAL_CHUNK)
                    pltpu.make_async_copy(fstage.at[p], o_hbm.at[pl.ds(tok0, FINAL_CHUNK), :], fsem.at[p]).start()
                return carry

            lax.fori_loop(0, tbt // FINAL_CHUNK, finish, 0)
            if tiled_out:
                pltpu.make_async_copy(ybuf.at[pl.ds(0, blk_rows), :], o_hbm.at[pl.ds(row0, blk_rows), :], osem).wait()
            else:
                for p in range(2):
                    pltpu.make_async_copy(fstage.at[p], o_hbm.at[pl.ds(0, FINAL_CHUNK), :], fsem.at[p]).wait()


def _experts(h1rt, te, nact, winfo, slots_blk, gates_blk, w1b, b1, w2b, b2, ln_g, ln_b, layer, *, tm, tbt, tiled_out):
    nt = te.shape[0]
    tpb = nt // nact.shape[0]
    n_rows = h1rt.shape[0]
    out_shape = (n_rows, LANES) if tiled_out else (n_rows // ROW_TILE_ROWS, D_MODEL)
    kern = functools.partial(_expert_kernel, tm=tm, tbt=tbt, tpb=tpb, layer=layer, tiled_out=tiled_out)
    per_block = lambda i, te, na, wi: (i // tpb, 0, 0)
    per_expert = lambda i, te, na, wi: (layer, te[i], 0, 0)
    const = lambda i, te, na, wi: (0, 0)
    grid_spec = pltpu.PrefetchScalarGridSpec(
        num_scalar_prefetch=3,
        grid=(nt,),
        in_specs=[
            pl.BlockSpec((1, 1, slots_blk.shape[2]), per_block, memory_space=pltpu.SMEM,
                         pipeline_mode=pl.Buffered(1)),
            pl.BlockSpec((1, 1, gates_blk.shape[2]), per_block, memory_space=pltpu.SMEM,
                         pipeline_mode=pl.Buffered(1)),
            pl.BlockSpec(memory_space=pl.ANY),
            pl.BlockSpec(memory_space=pl.ANY),
            pl.BlockSpec((None, None, 1, 2 * D_FF), per_expert),
            pl.BlockSpec(memory_space=pl.ANY),
            pl.BlockSpec((None, None, 1, D_MODEL), per_expert),
            pl.BlockSpec((1, D_MODEL), const),
            pl.BlockSpec((1, D_MODEL), const),
        ],
        out_specs=pl.BlockSpec(memory_space=pl.ANY),
        scratch_shapes=[
            pltpu.VMEM(((tbt + 1) * ROW_TILE_ROWS, LANES), F32),
            pltpu.VMEM(((tbt + 1) * ROW_TILE_ROWS, LANES), F32),
            pltpu.VMEM((2, tm * ROW_TILE_ROWS, LANES), F32),
            pltpu.VMEM((2, ROW_TILE_ROWS * STAGE_PITCH, LANES), F32),
            pltpu.VMEM((tm, 2 * D_FF), F32),
            pltpu.VMEM((2, D_MODEL, 2 * D_FF), BF16),
            pltpu.VMEM((2, D_FF, D_MODEL), BF16),
            pltpu.VMEM((2, FINAL_CHUNK, D_MODEL), F32),
            pltpu.SemaphoreType.DMA,
            pltpu.SemaphoreType.DMA,
            pltpu.SemaphoreType.DMA((2,)),
            pltpu.SemaphoreType.DMA((2,)),
        ],
    )
    return pl.pallas_call(
        kern,
        grid_spec=grid_spec,
        out_shape=jax.ShapeDtypeStruct(out_shape, F32),
        compiler_params=pltpu.CompilerParams(
            dimension_semantics=("arbitrary",), vmem_limit_bytes=VMEM_LIMIT_BYTES),
        name="experts",
    )(te, nact, winfo, slots_blk, gates_blk, h1rt, w1b, b1, w2b, b2, ln_g, ln_b)


def _key_shift(n, tm):
    return max((n * TOP_K - 1).bit_length(), (N_EXPERTS * tm - 1).bit_length()) + 1


def _routing_plan(keys, counts, n, tm, tbt):
    nb = n // tbt
    tpb = tbt * TOP_K // tm + N_EXPERTS
    shift = _key_shift(n, tm)
    flag = 1 << (shift - 1)
    need = (-counts) % tm
    b_ids = jnp.arange(nb, dtype=jnp.int32)[:, None, None]
    e_ids = jnp.arange(N_EXPERTS, dtype=jnp.int32)[None, :, None]
    j_ids = jnp.arange(tm, dtype=jnp.int32)[None, None, :]
    pad_keys = jnp.where(j_ids < need[:, :, None],
                         (b_ids * GROUP_SLOTS + e_ids) * (1 << shift) + flag + j_ids,
                         (b_ids * GROUP_SLOTS + N_EXPERTS) * (1 << shift) + flag + e_ids * tm + j_ids)
    ks = jnp.sort(jnp.concatenate([keys, pad_keys.reshape(-1)])).reshape(nb, tpb * tm)
    low = ks & (flag - 1)
    real = (ks & flag) == 0
    slot = jnp.where(real, low & (tbt * TOP_K - 1), tbt * TOP_K)
    edge = jnp.full((nb, tm), tbt * TOP_K, jnp.int32)
    slots_blk = jnp.concatenate([edge, slot, edge], axis=1).reshape(nb, 1, (tpb + 2) * tm)
    e_t = (ks[:, ::tm] >> shift) % GROUP_SLOTS
    active = e_t < N_EXPERTS
    nact = jnp.sum(active.astype(jnp.int32), axis=1)
    active = active.reshape(-1)
    te = jnp.where(active, e_t.reshape(-1), N_EXPERTS - 1)
    nt = nb * tpb
    tile = jnp.arange(nt, dtype=jnp.int32)
    prev_same = (tile % tpb != 0) & (jnp.roll(te, 1) == te) & jnp.roll(active, 1)
    gstart = active & ~prev_same
    gidx = jnp.cumsum(gstart.astype(jnp.int32)) - 1
    starts = jnp.where(gstart, tile, nt)
    nxt = jnp.flip(lax.cummin(jnp.flip(jnp.roll(starts, -1).at[-1].set(nt))))
    has_next = nxt < nt
    tnext = te[jnp.minimum(nxt, nt - 1)]
    half = jnp.sum(real.reshape(nt, tm).astype(jnp.int32), axis=1) <= tm // 2
    winfo = (gstart.astype(jnp.int32) | (has_next.astype(jnp.int32) << 1) | ((gidx & 1) << 2)
             | (half.astype(jnp.int32) << 3) | (tnext << 4))
    return te, nact, winfo, slots_blk


def _pick_tile(total, want):
    t = min(total, want)
    while total % t:
        t //= 2
    return t


def kernel(x, mem, ln_in_g, ln_in_b, mem_ln_g, mem_ln_b, w_in, conv_w, pool_w, pool_scale, gmlp_ln_g, gmlp_ln_b,
           gmlp_w, gmlp_b, w_mem_kv, group_norm_g, w_out, ln1_g, ln1_b, w_router, b_router, w1, b1, w2, b2,
           ln2_g, ln2_b):
    bsz, seq, d = x.shape
    depth = w_in.shape[0]
    n = bsz * seq
    dg = D_GROUP
    ts = _pick_tile(seq, 512)
    tm = EXPERT_TILE
    tbt = _pick_tile(n, EXPERT_BLOCK_TOKENS)
    nb = n // tbt
    assert N_EXPERTS == 32 and tbt % ts == 0 and tbt & (tbt - 1) == 0 and tbt % FINAL_CHUNK == 0
    key_shift = _key_shift(n, tm)
    block_shift = tbt.bit_length() - 1

    h = _ln_rows(x.reshape(n, d), ln_in_g, ln_in_b, _pick_tile(n, 1024))

    wkv_all = jnp.transpose(w_mem_kv, (1, 0, 2)).reshape(d, depth * 2 * dg).astype(BF16)
    kv_all = _mem_kv(mem, mem_ln_g, mem_ln_b, wkv_all)

    pos = jnp.arange(GMLP_BLOCK)
    mask = (pos[None, :] // CHUNK) <= (pos[:, None] // CHUNK)
    eye = jnp.eye(len(POOL_WINDOWS), dtype=F32)

    wr_pad = jnp.zeros((depth, d, LANES), F32).at[:, :, :N_EXPERTS].set(w_router)
    wr_hi = wr_pad.astype(BF16)
    wr_lo = (wr_pad - wr_hi.astype(F32)).astype(BF16)
    br_pad = jnp.full((depth, 1, LANES), NEG_BIG, F32).at[:, 0, :N_EXPERTS].set(b_router)
    w1b = w1.astype(BF16)
    w2b = w2.astype(BF16)
    b1r = b1.reshape(depth, N_EXPERTS, 1, 2 * D_FF)
    b2r = b2.reshape(depth, N_EXPERTS, 1, d)

    for l in range(depth):
        pool_bd = jnp.einsum('gh,gcd->gchd', eye, pool_w[l]).reshape(dg, dg).astype(BF16)
        gw = jnp.where(mask[None], gmlp_w[l], 0.0).reshape(GMLP_HEADS * GMLP_BLOCK, GMLP_BLOCK).astype(BF16)
        gbias = jnp.repeat(gmlp_b[l].T, GMLP_HEAD_DIM, axis=1)
        lw = (w_in[l].astype(BF16), conv_w[l], pool_bd, pool_scale[l].reshape(1, dg),
              gmlp_ln_g[l].reshape(1, dg), gmlp_ln_b[l].reshape(1, dg), gw, gbias,
              group_norm_g[l].reshape(1, d), w_out[l].astype(BF16),
              ln1_g[l].reshape(1, d), ln1_b[l].reshape(1, d), wr_hi[l], wr_lo[l], br_pad[l])
        h1rt, keys, gates, cnt = _mixer(h, kv_all, l, lw, bsz=bsz, seq=seq, ts=ts,
                                        key_shift=key_shift, block_shift=block_shift)
        counts = jnp.sum(cnt.reshape(nb, tbt // ts, LANES), axis=1)[:, :N_EXPERTS].astype(jnp.int32)
        te, nact, winfo, slots_blk = _routing_plan(keys[:, :TOP_K].reshape(-1), counts, n, tm, tbt)
        gates_blk = jnp.pad(gates[:, :TOP_K].reshape(nb, 1, tbt * TOP_K), ((0, 0), (0, 0), (0, LANES)))
        h = _experts(h1rt, te, nact, winfo, slots_blk, gates_blk, w1b, b1r, w2b, b2r,
                     ln2_g[l].reshape(1, d), ln2_b[l].reshape(1, d), l,
                     tm=tm, tbt=tbt, tiled_out=l + 1 < depth)
    return h.reshape(bsz, seq, d)
```

```python
import functools

import jax
import jax.numpy as jnp
from jax import lax
from jax.experimental import pallas as pl
from jax.experimental.pallas import tpu as pltpu

F32 = jnp.float32
BF16 = jnp.bfloat16

D_MODEL = 1024
N_GROUPS = 4
D_GROUP = D_MODEL // N_GROUPS
D_IN_PROJ = 7 * D_GROUP
POOL_WINDOWS = (2, 4, 8, 16)
POOL_CH = D_GROUP // len(POOL_WINDOWS)
GMLP_BLOCK = 128
GMLP_HEADS = 4
GMLP_HEAD_DIM = D_GROUP // GMLP_HEADS
CHUNK = 64
MEM_HEADS = 4
MEM_HEAD_DIM = D_GROUP // MEM_HEADS
N_EXPERTS = 32
GROUP_SLOTS = N_EXPERTS + 1
TOP_K = 4
D_FF = D_MODEL
SWIGLU_ALPHA = 1.702
SWIGLU_LIMIT = 7.0
REF_DEPTH = 4
DEEPNORM_ALPHA = (2 * REF_DEPTH) ** 0.25
LN_EPS = 1e-5
RMS_EPS = 1e-6

SUBLANES = 8
LANES = 128
ROW_TILE_ROWS = D_MODEL // LANES
HALO = max(POOL_WINDOWS)
VMEM_LIMIT_BYTES = 58 * 1024 * 1024
EXPERT_CHUNKS = 4
EXPERT_TILE = 256
EXPERT_BLOCK_TOKENS = 4096
STAGE_PITCH = EXPERT_TILE + SUBLANES
FINAL_CHUNK = 128
ROW_BATCH = 8
WEIGHT_DMA_PARTS = 4
ROW_QUARTERS = 4

NEG_BIG = -1e30


def _layer_norm(x, g, b):
    mu = jnp.mean(x, axis=-1, keepdims=True)
    xc = x - mu
    var = jnp.mean(xc * xc, axis=-1, keepdims=True)
    return xc * lax.rsqrt(var + LN_EPS) * g + b


def _ln_kernel(x_ref, g_ref, b_ref, o_ref, *, rows):
    y = _layer_norm(x_ref[...], g_ref[...], b_ref[...])
    for j in range(ROW_TILE_ROWS):
        o_ref[pl.ds(j, rows, stride=ROW_TILE_ROWS), :] = y[:, j * LANES:(j + 1) * LANES]


def _ln_rows(x, g, b, rows):
    n, d = x.shape
    return pl.pallas_call(
        functools.partial(_ln_kernel, rows=rows),
        grid=(n // rows,),
        in_specs=[pl.BlockSpec((rows, d), lambda i: (i, 0)),
                  pl.BlockSpec((1, d), lambda i: (0, 0)),
                  pl.BlockSpec((1, d), lambda i: (0, 0))],
        out_specs=pl.BlockSpec((rows * ROW_TILE_ROWS, LANES), lambda i: (i, 0)),
        out_shape=jax.ShapeDtypeStruct((n * ROW_TILE_ROWS, LANES), F32),
        name="entry_ln",
    )(x, g.reshape(1, d), b.reshape(1, d))


def _mem_kv_kernel(m_ref, g_ref, b_ref, w_ref, o_ref):
    mn = _layer_norm(m_ref[0], g_ref[...], b_ref[...])
    o_ref[0] = jnp.dot(mn.astype(BF16), w_ref[...], preferred_element_type=F32).astype(BF16)


def _mem_kv(mem, g, b, w_all):
    bsz, m, d = mem.shape
    p = w_all.shape[1]
    return pl.pallas_call(
        _mem_kv_kernel,
        grid=(bsz,),
        in_specs=[pl.BlockSpec((1, m, d), lambda i: (i, 0, 0)),
                  pl.BlockSpec((1, d), lambda i: (0, 0)),
                  pl.BlockSpec((1, d), lambda i: (0, 0)),
                  pl.BlockSpec((d, p), lambda i: (0, 0))],
        out_specs=pl.BlockSpec((1, m, p), lambda i: (i, 0, 0)),
        out_shape=jax.ShapeDtypeStruct((bsz, m, p), BF16),
        name="mem_kv",
    )(mem, g.reshape(1, d), b.reshape(1, d), w_all)


def _mixer_kernel(x_ref, kv_ref, win_ref, convw_ref, poolw_ref, pscale_ref, glng_ref, glnb_ref,
                  gw_ref, gbias_ref, gng_ref, wout_ref, ln1g_ref, ln1b_ref, wrh_ref, wrl_ref, br_ref,
                  h1rt_ref, key_ref, gate_ref, cnt_ref, ext_ref, *, ts, key_shift, block_shift):
    s = pl.program_id(1)
    dg = D_GROUP

    x = jnp.concatenate([x_ref[pl.ds(j, ts, stride=ROW_TILE_ROWS), :] for j in range(ROW_TILE_ROWS)], axis=1)
    proj = jnp.dot(x.astype(BF16), win_ref[...], preferred_element_type=F32)
    gate_b = proj[:, 0:dg]
    gate_c = proj[:, dg:2 * dg]
    conv_in = proj[:, 2 * dg:3 * dg]
    pool_in = proj[:, 3 * dg:4 * dg]
    gmlp_in = proj[:, 4 * dg:6 * dg]
    mem_q = proj[:, 6 * dg:7 * dg]

    @pl.when(s == 0)
    def _():
        ext_ref[0:HALO, :] = jnp.zeros((HALO, 2 * dg), F32)

    z = gate_c * conv_in
    ext_ref[HALO:HALO + ts, 0:dg] = z
    ext_ref[HALO:HALO + ts, dg:2 * dg] = pool_in

    zm1 = ext_ref[pl.ds(HALO - 1, ts), 0:dg]
    zm2 = ext_ref[pl.ds(HALO - 2, ts), 0:dg]
    cw = convw_ref[...]
    y_conv = gate_b * (cw[0:1, :] * zm2 + cw[1:2, :] * zm1 + cw[2:3, :] * z)

    pos = s * ts + lax.broadcasted_iota(jnp.int32, (ts, 1), 0)
    half = dg // 2
    lane_h = lax.broadcasted_iota(jnp.int32, (ts, half), 1)
    diffs = []
    for hf in range(2):
        c0 = dg + hf * half
        w_lo, w_hi = POOL_WINDOWS[2 * hf], POOL_WINDOWS[2 * hf + 1]
        cur = ext_ref[pl.ds(HALO, ts), c0:c0 + half]
        acc = cur
        for k in range(1, w_lo):
            acc = acc + ext_ref[pl.ds(HALO - k, ts), c0:c0 + half]
        s_lo = acc
        for k in range(w_lo, w_hi):
            acc = acc + ext_ref[pl.ds(HALO - k, ts), c0:c0 + half]
        s_hi = acc
        cnt_lo = jnp.minimum(pos + 1, w_lo).astype(F32)
        cnt_hi = jnp.minimum(pos + 1, w_hi).astype(F32)
        mean = jnp.where(lane_h < POOL_CH, s_lo / cnt_lo, s_hi / cnt_hi)
        diffs.append(mean - cur)
    dpool = jnp.concatenate(diffs, axis=1)
    y_pool = jnp.dot(dpool.astype(BF16), poolw_ref[...], preferred_element_type=F32) * pscale_ref[...]

    ext_ref[0:HALO, :] = ext_ref[ts:ts + HALO, :]

    zg = jax.nn.gelu(gmlp_in, approximate=True)
    u = zg[:, 0:dg]
    v = _layer_norm(zg[:, dg:2 * dg], glng_ref[...], glnb_ref[...]).astype(BF16)
    lane_g = lax.broadcasted_iota(jnp.int32, (GMLP_BLOCK, dg), 1)
    gw = gw_ref[...]
    gbias = gbias_ref[...]
    gates = []
    for nb in range(ts // GMLP_BLOCK):
        vb = v[nb * GMLP_BLOCK:(nb + 1) * GMLP_BLOCK, :]
        r = jnp.dot(gw, vb, preferred_element_type=F32)
        g = r[0:GMLP_BLOCK, :]
        for h in range(1, GMLP_HEADS):
            g = jnp.where(lane_g >= h * GMLP_HEAD_DIM, r[h * GMLP_BLOCK:(h + 1) * GMLP_BLOCK, :], g)
        gates.append(g + gbias)
    y_gmlp = u * jnp.concatenate(gates, axis=0)

    kv = kv_ref[0]
    mk = kv[:, 0:dg]
    mv = kv[:, dg:2 * dg]
    lane_q = lax.broadcasted_iota(jnp.int32, (ts, dg), 1)
    qs = mem_q * (MEM_HEAD_DIM ** -0.5)
    y_mem = jnp.zeros((ts, dg), F32)
    for h in range(MEM_HEADS):
        in_head = (lane_q >= h * MEM_HEAD_DIM) & (lane_q < (h + 1) * MEM_HEAD_DIM)
        qh = jnp.where(in_head, qs, 0.0).astype(BF16)
        sc = lax.dot_general(qh, mk, (((1,), (1,)), ((), ())), preferred_element_type=F32)
        e = jnp.exp(sc - jnp.max(sc, axis=-1, keepdims=True))
        den = jnp.sum(e, axis=-1, keepdims=True)
        oh = jnp.dot(e.astype(BF16), mv, preferred_element_type=F32) / den
        y_mem = jnp.where(in_head, oh, y_mem)

    gng = gng_ref[...]
    groups = []
    for gi, y in enumerate((y_conv, y_pool, y_gmlp, y_mem)):
        ms = jnp.mean(y * y, axis=-1, keepdims=True)
        groups.append(y * lax.rsqrt(ms + RMS_EPS) * gng[:, gi * dg:(gi + 1) * dg])
    cat = jnp.concatenate(groups, axis=1).astype(BF16)
    mix = jnp.dot(cat, wout_ref[...], preferred_element_type=F32)

    h1 = _layer_norm(DEEPNORM_ALPHA * x + mix, ln1g_ref[...], ln1b_ref[...])
    for j in range(ROW_TILE_ROWS):
        h1rt_ref[pl.ds(j, ts, stride=ROW_TILE_ROWS), :] = h1[:, j * LANES:(j + 1) * LANES]

    h1_hi = h1.astype(BF16)
    h1_lo = (h1 - h1_hi.astype(F32)).astype(BF16)
    both = jnp.dot(h1_hi, wrl_ref[...], preferred_element_type=F32)
    logits = (both + pltpu.roll(both, LANES - N_EXPERTS, axis=1)
              + jnp.dot(h1_lo, wrh_ref[...], preferred_element_type=F32)) + br_ref[...]
    lane = lax.broadcasted_iota(jnp.int32, (ts, LANES), 1)
    lane_f = lane.astype(F32)
    vals, idxs = [], []
    cur_l = logits
    for _ in range(TOP_K):
        m = jnp.max(cur_l, axis=-1, keepdims=True)
        i_f = jnp.min(jnp.where(cur_l == m, lane_f, float(LANES)), axis=-1, keepdims=True)
        vals.append(m)
        idxs.append(i_f)
        cur_l = jnp.where(lane_f == i_f, -jnp.inf, cur_l)
    es = [jnp.exp(vk - vals[0]) for vk in vals]
    den = es[0] + es[1] + es[2] + es[3]
    idx_out = jnp.zeros((ts, LANES), F32)
    gate_out = jnp.zeros((ts, LANES), F32)
    hits = jnp.zeros((ts, LANES), F32)
    for k in range(TOP_K):
        idx_out = jnp.where(lane == k, idxs[k], idx_out)
        gate_out = jnp.where(lane == k, es[k] / den, gate_out)
        hits = hits + jnp.where(lane_f == idxs[k], 1.0, 0.0)
    tok = (pl.program_id(0) * pl.num_programs(1) + s) * ts + lax.broadcasted_iota(jnp.int32, (ts, LANES), 0)
    group = (tok >> block_shift) * GROUP_SLOTS + idx_out.astype(jnp.int32)
    key_ref[...] = jnp.where(lane < TOP_K, group * (1 << key_shift) + tok * TOP_K + lane, 0)
    gate_ref[...] = gate_out
    cnt_ref[0] = jnp.sum(hits, axis=0, keepdims=True)


def _mixer(h, kv_all, layer, lw, *, bsz, seq, ts, key_shift, block_shift):
    n = bsz * seq
    st = seq // ts
    dg = D_GROUP
    const = lambda b, s: (0, 0)
    kern = functools.partial(_mixer_kernel, ts=ts, key_shift=key_shift, block_shift=block_shift)
    return pl.pallas_call(
        kern,
        grid=(bsz, st),
        in_specs=[
            pl.BlockSpec((ts * ROW_TILE_ROWS, LANES), lambda b, s: (b * st + s, 0)),
            pl.BlockSpec((1, kv_all.shape[1], 2 * dg), lambda b, s: (b, 0, layer)),
            pl.BlockSpec((D_MODEL, D_IN_PROJ), const),
            pl.BlockSpec((3, dg), const),
            pl.BlockSpec((dg, dg), const),
            pl.BlockSpec((1, dg), const),
            pl.BlockSpec((1, dg), const),
            pl.BlockSpec((1, dg), const),
            pl.BlockSpec((GMLP_HEADS * GMLP_BLOCK, GMLP_BLOCK), const),
            pl.BlockSpec((GMLP_BLOCK, dg), const),
            pl.BlockSpec((1, D_MODEL), const),
            pl.BlockSpec((D_MODEL, D_MODEL), const),
            pl.BlockSpec((1, D_MODEL), const),
            pl.BlockSpec((1, D_MODEL), const),
            pl.BlockSpec((D_MODEL, LANES), const),
            pl.BlockSpec((D_MODEL, LANES), const),
            pl.BlockSpec((1, LANES), const),
        ],
        out_specs=[
            pl.BlockSpec((ts * ROW_TILE_ROWS, LANES), lambda b, s: (b * st + s, 0)),
            pl.BlockSpec((ts, LANES), lambda b, s: (b * st + s, 0)),
            pl.BlockSpec((ts, LANES), lambda b, s: (b * st + s, 0)),
            pl.BlockSpec((1, 1, LANES), lambda b, s: (b * st + s, 0, 0)),
        ],
        out_shape=[
            jax.ShapeDtypeStruct((n * ROW_TILE_ROWS, LANES), F32),
            jax.ShapeDtypeStruct((n, LANES), jnp.int32),
            jax.ShapeDtypeStruct((n, LANES), F32),
            jax.ShapeDtypeStruct((bsz * st, 1, LANES), F32),
        ],
        scratch_shapes=[pltpu.VMEM((ts + HALO, 2 * dg), F32)],
        compiler_params=pltpu.CompilerParams(
            dimension_semantics=("arbitrary", "arbitrary"), vmem_limit_bytes=VMEM_LIMIT_BYTES),
        name="mixer",
    )(h, kv_all, *lw)


def _expert_kernel(te_ref, nact_ref, winfo_ref, slot_ref, gate_ref,
                   h_hbm, w1_hbm, b1_ref, w2_hbm, b2_ref, g_ref, bt_ref, o_hbm,
                   xblk, ybuf, sin, sout, hbuf, w1buf, w2buf, fstage, xsem, osem, wsem, fsem,
                   *, tm, tbt, tpb, layer, tiled_out):
    i = pl.program_id(0)
    b = i // tpb
    tj = i - b * tpb
    n_act = nact_ref[b]
    par = lax.rem(i, 2)
    rt = ROW_TILE_ROWS
    blk_rows = tbt * rt
    sp = STAGE_PITCH
    cw = D_FF // EXPERT_CHUNKS
    base = (tj + 1) * tm

    def zero_row(tiles, width):
        bits = pltpu.bitcast(tiles[0], jnp.uint32)
        for t in tiles[1:]:
            bits = bits | pltpu.bitcast(t, jnp.uint32)
        z = ((bits >> 16) >> 16).astype(F32)[0:1, :]
        return jnp.concatenate([z] * (width // LANES), axis=1)

    def gather(first_entry, dst, lo=0, hi=tm):
        tiles = []
        entries = slot_ref.at[0, 0, pl.ds(first_entry, tm)]
        for r in range(lo, hi):
            off = pl.multiple_of((entries[r] >> 2) * rt, rt)
            tiles.append(xblk[pl.ds(off, rt), :])
            dst[pl.ds(r * rt, rt), :] = tiles[-1]
        return tiles

    def accumulate(first_entry, src, lo=0, hi=tm):
        sums = []
        entries = slot_ref.at[0, 0, pl.ds(first_entry, tm)]
        for r0 in range(lo, hi, ROW_BATCH):
            rs = range(r0, r0 + ROW_BATCH)
            slots = [entries[r] for r in rs]
            offs = [pl.multiple_of((s >> 2) * rt, rt) for s in slots]
            accs = [ybuf[pl.ds(o, rt), :] for o in offs]
            vals = [src[pl.ds(r, rt, stride=sp), :] for r in rs]
            for k, r in enumerate(rs):
                gate = gate_ref[0, 0, slots[k]]
                sums.append(accs[k] + gate * vals[k])
                ybuf[pl.ds(offs[k], rt), :] = sums[-1]
        return sums

    @pl.when(i == 0)
    def _():
        xblk[pl.ds(blk_rows, rt), :] = jnp.zeros((rt, LANES), F32)
        sout[...] = jnp.zeros(sout.shape, F32)

    def weight_copies(e, wslot):
        cps = []
        for src, dst in ((w1_hbm, w1buf), (w2_hbm, w2buf)):
            rows_p = src.shape[2] // WEIGHT_DMA_PARTS
            for p in range(WEIGHT_DMA_PARTS):
                sl = pl.ds(p * rows_p, rows_p)
                cps.append((pltpu.make_async_copy(src.at[layer, e, sl], dst.at[wslot, sl], wsem.at[wslot]), p % 2))
        return cps

    @pl.when(tj < n_act)
    def _():
        first = tj == 0
        last = tj == n_act - 1
        row0 = pl.multiple_of(b * blk_rows, blk_rows)
        info = winfo_ref[i]
        wpar = (info >> 2) & 1

        @pl.when(i == 0)
        def _():
            for cp, prio in weight_copies(te_ref[0], 0):
                cp.start(priority=prio)

        @pl.when((info & 1) == 1)
        def _():
            for cp, _ in weight_copies(te_ref[i], wpar):
                cp.wait()

            @pl.when((info & 2) == 2)
            def _():
                for cp, prio in weight_copies(info >> 5, 1 - wpar):
                    cp.start(priority=prio)

        w1_ref = w1buf.at[wpar]
        w2_ref = w2buf.at[wpar]

        @pl.when(first)
        def _():
            load = pltpu.make_async_copy(h_hbm.at[pl.ds(row0, blk_rows), :], xblk.at[pl.ds(0, blk_rows), :], xsem)
            load.start()
            zrows = FINAL_CHUNK * rt

            def zero(c, carry):
                ybuf[pl.ds(pl.multiple_of(c * zrows, zrows), zrows), :] = jnp.zeros((zrows, LANES), F32)
                return carry

            lax.fori_loop(0, tbt // FINAL_CHUNK, zero, 0)
            ybuf[pl.ds(blk_rows, rt), :] = jnp.zeros((rt, LANES), F32)
            load.wait()
            gather(base, sin.at[par])

        so = sout.at[par]

        def expert(m):
            xs = sin.at[par]
            x = jnp.concatenate([xs[pl.ds(j, m, stride=rt), :] for j in range(rt)], axis=1)
            xb = x.astype(BF16)
            per1 = tm // (2 * EXPERT_CHUNKS)
            per2 = tm // EXPERT_CHUNKS
            for c in range(2 * EXPERT_CHUNKS):
                tie = zero_row(accumulate(base - tm, sout.at[1 - par], c * per1, (c + 1) * per1), cw)
                hbuf[0:m, c * cw:(c + 1) * cw] = (
                    jnp.dot(xb, w1_ref[:, c * cw:(c + 1) * cw], preferred_element_type=F32)
                    + (b1_ref[:, c * cw:(c + 1) * cw] + tie))
            out = None
            for c in range(EXPERT_CHUNKS):
                tie = zero_row(gather(base + tm, sin.at[1 - par], c * per2, (c + 1) * per2), cw)
                glu = jnp.minimum(hbuf[0:m, c * cw:(c + 1) * cw], SWIGLU_LIMIT)
                lin = jnp.clip(hbuf[0:m, D_FF + c * cw:D_FF + (c + 1) * cw] + tie, -SWIGLU_LIMIT, SWIGLU_LIMIT)
                act = glu * jax.nn.sigmoid(SWIGLU_ALPHA * glu) * (lin + 1.0)
                part = jnp.dot(act.astype(BF16), w2_ref[c * cw:(c + 1) * cw, :], preferred_element_type=F32)
                out = part if out is None else out + part
            out = out + b2_ref[...]
            for j in range(rt):
                so[pl.ds(j * sp, m), :] = out[:, j * LANES:(j + 1) * LANES]

        quarters = (info >> 3) & 3
        for q in range(ROW_QUARTERS):
            pl.when(quarters == q)(functools.partial(expert, (q + 1) * tm // ROW_QUARTERS))

        @pl.when(last)
        def _():
            accumulate(base, so)
            crows = FINAL_CHUNK * rt

            def finish(c, carry):
                c0 = pl.multiple_of(c * crows, crows)
                r = jnp.concatenate(
                    [DEEPNORM_ALPHA * xblk[pl.ds(c0 + jj, FINAL_CHUNK, stride=rt), :]
                     + ybuf[pl.ds(c0 + jj, FINAL_CHUNK, stride=rt), :] for jj in range(rt)], axis=1)
                y = _layer_norm(r, g_ref[...], bt_ref[...])
                if tiled_out:
                    for jj in range(rt):
                        ybuf[pl.ds(c0 + jj, FINAL_CHUNK, stride=rt), :] = y[:, jj * LANES:(jj + 1) * LANES]
                    pltpu.make_async_copy(
                        ybuf.at[pl.ds(c0, crows), :],
                        o_hbm.at[pl.ds(pl.multiple_of(row0 + c * crows, crows), crows), :], osem).start()
                else:
                    p = lax.rem(c, 2)

                    @pl.when(c >= 2)
                    def _():
                        pltpu.make_async_copy(fstage.at[p], o_hbm.at[pl.ds(0, FINAL_CHUNK), :], fsem.at[p]).wait()

                    fstage[p] = y
                    tok0 = pl.multiple_of(b * tbt + c * FINAL_CHUNK, FINAL_CHUNK)
                    pltpu.make_async_copy(fstage.at[p], o_hbm.at[pl.ds(tok0, FINAL_CHUNK), :], fsem.at[p]).start()
                return carry

            lax.fori_loop(0, tbt // FINAL_CHUNK, finish, 0)
            if tiled_out:
                pltpu.make_async_copy(ybuf.at[pl.ds(0, blk_rows), :], o_hbm.at[pl.ds(row0, blk_rows), :], osem).wait()
            else:
                for p in range(2):
                    pltpu.make_async_copy(fstage.at[p], o_hbm.at[pl.ds(0, FINAL_CHUNK), :], fsem.at[p]).wait()


def _experts(h1rt, te, nact, winfo, slots_blk, gates_blk, w1b, b1, w2b, b2, ln_g, ln_b, layer, *, tm, tbt, tiled_out):
    nt = te.shape[0]
    tpb = nt // nact.shape[0]
    n_rows = h1rt.shape[0]
    out_shape = (n_rows, LANES) if tiled_out else (n_rows // ROW_TILE_ROWS, D_MODEL)
    kern = functools.partial(_expert_kernel, tm=tm, tbt=tbt, tpb=tpb, layer=layer, tiled_out=tiled_out)
    per_block = lambda i, te, na, wi: (i // tpb, 0, 0)
    per_expert = lambda i, te, na, wi: (layer, te[i], 0, 0)
    const = lambda i, te, na, wi: (0, 0)
    grid_spec = pltpu.PrefetchScalarGridSpec(
        num_scalar_prefetch=3,
        grid=(nt,),
        in_specs=[
            pl.BlockSpec((1, 1, slots_blk.shape[2]), per_block, memory_space=pltpu.SMEM,
                         pipeline_mode=pl.Buffered(1)),
            pl.BlockSpec((1, 1, gates_blk.shape[2]), per_block, memory_space=pltpu.SMEM,
                         pipeline_mode=pl.Buffered(1)),
            pl.BlockSpec(memory_space=pl.ANY),
            pl.BlockSpec(memory_space=pl.ANY),
            pl.BlockSpec((None, None, 1, 2 * D_FF), per_expert),
            pl.BlockSpec(memory_space=pl.ANY),
            pl.BlockSpec((None, None, 1, D_MODEL), per_expert),
            pl.BlockSpec((1, D_MODEL), const),
            pl.BlockSpec((1, D_MODEL), const),
        ],
        out_specs=pl.BlockSpec(memory_space=pl.ANY),
        scratch_shapes=[
            pltpu.VMEM(((tbt + 1) * ROW_TILE_ROWS, LANES), F32),
            pltpu.VMEM(((tbt + 1) * ROW_TILE_ROWS, LANES), F32),
            pltpu.VMEM((2, tm * ROW_TILE_ROWS, LANES), F32),
            pltpu.VMEM((2, ROW_TILE_ROWS * STAGE_PITCH, LANES), F32),
            pltpu.VMEM((tm, 2 * D_FF), F32),
            pltpu.VMEM((2, D_MODEL, 2 * D_FF), BF16),
            pltpu.VMEM((2, D_FF, D_MODEL), BF16),
            pltpu.VMEM((2, FINAL_CHUNK, D_MODEL), F32),
            pltpu.SemaphoreType.DMA,
            pltpu.SemaphoreType.DMA,
            pltpu.SemaphoreType.DMA((2,)),
            pltpu.SemaphoreType.DMA((2,)),
        ],
    )
    return pl.pallas_call(
        kern,
        grid_spec=grid_spec,
        out_shape=jax.ShapeDtypeStruct(out_shape, F32),
        compiler_params=pltpu.CompilerParams(
            dimension_semantics=("arbitrary",), vmem_limit_bytes=VMEM_LIMIT_BYTES),
        name="experts",
    )(te, nact, winfo, slots_blk, gates_blk, h1rt, w1b, b1, w2b, b2, ln_g, ln_b)


def _key_shift(n, tm):
    return max((n * TOP_K - 1).bit_length(), (N_EXPERTS * tm - 1).bit_length()) + 1


def _routing_plan(keys, counts, n, tm, tbt):
    nb = n // tbt
    tpb = tbt * TOP_K // tm + N_EXPERTS
    shift = _key_shift(n, tm)
    flag = 1 << (shift - 1)
    need = (-counts) % tm
    b_ids = jnp.arange(nb, dtype=jnp.int32)[:, None, None]
    e_ids = jnp.arange(N_EXPERTS, dtype=jnp.int32)[None, :, None]
    j_ids = jnp.arange(tm, dtype=jnp.int32)[None, None, :]
    pad_keys = jnp.where(j_ids < need[:, :, None],
                         (b_ids * GROUP_SLOTS + e_ids) * (1 << shift) + flag + j_ids,
                         (b_ids * GROUP_SLOTS + N_EXPERTS) * (1 << shift) + flag + e_ids * tm + j_ids)
    ks = jnp.sort(jnp.concatenate([keys, pad_keys.reshape(-1)])).reshape(nb, tpb * tm)
    low = ks & (flag - 1)
    real = (ks & flag) == 0
    slot = jnp.where(real, low & (tbt * TOP_K - 1), tbt * TOP_K)
    edge = jnp.full((nb, tm), tbt * TOP_K, jnp.int32)
    slots_blk = jnp.concatenate([edge, slot, edge], axis=1).reshape(nb, 1, (tpb + 2) * tm)
    e_t = (ks[:, ::tm] >> shift) % GROUP_SLOTS
    active = e_t < N_EXPERTS
    nact = jnp.sum(active.astype(jnp.int32), axis=1)
    active = active.reshape(-1)
    te = jnp.where(active, e_t.reshape(-1), N_EXPERTS - 1)
    nt = nb * tpb
    tile = jnp.arange(nt, dtype=jnp.int32)
    prev_same = (tile % tpb != 0) & (jnp.roll(te, 1) == te) & jnp.roll(active, 1)
    gstart = active & ~prev_same
    gidx = jnp.cumsum(gstart.astype(jnp.int32)) - 1
    starts = jnp.where(gstart, tile, nt)
    nxt = jnp.flip(lax.cummin(jnp.flip(jnp.roll(starts, -1).at[-1].set(nt))))
    has_next = nxt < nt
    tnext = te[jnp.minimum(nxt, nt - 1)]
    n_real = jnp.sum(real.reshape(nt, tm).astype(jnp.int32), axis=1)
    quarters = jnp.clip((n_real + tm // ROW_QUARTERS - 1) // (tm // ROW_QUARTERS), 1, ROW_QUARTERS) - 1
    winfo = (gstart.astype(jnp.int32) | (has_next.astype(jnp.int32) << 1) | ((gidx & 1) << 2)
             | (quarters << 3) | (tnext << 5))
    return te, nact, winfo, slots_blk


def _pick_tile(total, want):
    t = min(total, want)
    while total % t:
        t //= 2
    return t


def kernel(x, mem, ln_in_g, ln_in_b, mem_ln_g, mem_ln_b, w_in, conv_w, pool_w, pool_scale, gmlp_ln_g, gmlp_ln_b,
           gmlp_w, gmlp_b, w_mem_kv, group_norm_g, w_out, ln1_g, ln1_b, w_router, b_router, w1, b1, w2, b2,
           ln2_g, ln2_b):
    bsz, seq, d = x.shape
    depth = w_in.shape[0]
    n = bsz * seq
    dg = D_GROUP
    ts = _pick_tile(seq, 512)
    tm = EXPERT_TILE
    tbt = _pick_tile(n, EXPERT_BLOCK_TOKENS)
    nb = n // tbt
    assert N_EXPERTS == 32 and tbt % ts == 0 and tbt & (tbt - 1) == 0 and tbt % FINAL_CHUNK == 0
    key_shift = _key_shift(n, tm)
    block_shift = tbt.bit_length() - 1

    h = _ln_rows(x.reshape(n, d), ln_in_g, ln_in_b, _pick_tile(n, 1024))

    wkv_all = jnp.transpose(w_mem_kv, (1, 0, 2)).reshape(d, depth * 2 * dg).astype(BF16)
    kv_all = _mem_kv(mem, mem_ln_g, mem_ln_b, wkv_all)

    pos = jnp.arange(GMLP_BLOCK)
    mask = (pos[None, :] // CHUNK) <= (pos[:, None] // CHUNK)
    eye = jnp.eye(len(POOL_WINDOWS), dtype=F32)

    wr_pad = jnp.zeros((depth, d, LANES), F32).at[:, :, :N_EXPERTS].set(w_router)
    wr_hi = wr_pad.astype(BF16)
    wr_lo = wr_hi + jnp.roll((wr_pad - wr_hi.astype(F32)).astype(BF16), N_EXPERTS, axis=2)
    br_pad = jnp.full((depth, 1, LANES), NEG_BIG, F32).at[:, 0, :N_EXPERTS].set(b_router)
    w1b = w1.astype(BF16)
    w2b = w2.astype(BF16)
    b1r = b1.reshape(depth, N_EXPERTS, 1, 2 * D_FF)
    b2r = b2.reshape(depth, N_EXPERTS, 1, d)

    for l in range(depth):
        pool_bd = jnp.einsum('gh,gcd->gchd', eye, pool_w[l]).reshape(dg, dg).astype(BF16)
        gw = jnp.where(mask[None], gmlp_w[l], 0.0).reshape(GMLP_HEADS * GMLP_BLOCK, GMLP_BLOCK).astype(BF16)
        gbias = jnp.repeat(gmlp_b[l].T, GMLP_HEAD_DIM, axis=1)
        lw = (w_in[l].astype(BF16), conv_w[l], pool_bd, pool_scale[l].reshape(1, dg),
              gmlp_ln_g[l].reshape(1, dg), gmlp_ln_b[l].reshape(1, dg), gw, gbias,
              group_norm_g[l].reshape(1, d), w_out[l].astype(BF16),
              ln1_g[l].reshape(1, d), ln1_b[l].reshape(1, d), wr_hi[l], wr_lo[l], br_pad[l])
        h1rt, keys, gates, cnt = _mixer(h, kv_all, l, lw, bsz=bsz, seq=seq, ts=ts,
                                        key_shift=key_shift, block_shift=block_shift)
        counts = jnp.sum(cnt.reshape(nb, tbt // ts, LANES), axis=1)[:, :N_EXPERTS].astype(jnp.int32)
        te, nact, winfo, slots_blk = _routing_plan(keys[:, :TOP_K].reshape(-1), counts, n, tm, tbt)
        gates_blk = jnp.pad(gates[:, :TOP_K].reshape(nb, 1, tbt * TOP_K), ((0, 0), (0, 0), (0, LANES)))
        h = _experts(h1rt, te, nact, winfo, slots_blk, gates_blk, w1b, b1r, w2b, b2r,
                     ln2_g[l].reshape(1, d), ln2_b[l].reshape(1, d), l,
                     tm=tm, tbt=tbt, tiled_out=l + 1 < depth)
    return h.reshape(bsz, seq, d)
```

```python
import functools

import jax
import jax.numpy as jnp
from jax import lax
from jax.experimental import pallas as pl
from jax.experimental.pallas import tpu as pltpu

F32 = jnp.float32
BF16 = jnp.bfloat16

D_MODEL = 1024
N_GROUPS = 4
D_GROUP = D_MODEL // N_GROUPS
D_IN_PROJ = 7 * D_GROUP
POOL_WINDOWS = (2, 4, 8, 16)
POOL_CH = D_GROUP // len(POOL_WINDOWS)
GMLP_BLOCK = 128
GMLP_HEADS = 4
GMLP_HEAD_DIM = D_GROUP // GMLP_HEADS
CHUNK = 64
MEM_HEADS = 4
MEM_HEAD_DIM = D_GROUP // MEM_HEADS
N_EXPERTS = 32
GROUP_SLOTS = N_EXPERTS + 1
TOP_K = 4
D_FF = D_MODEL
SWIGLU_ALPHA = 1.702
SWIGLU_LIMIT = 7.0
REF_DEPTH = 4
DEEPNORM_ALPHA = (2 * REF_DEPTH) ** 0.25
LN_EPS = 1e-5
RMS_EPS = 1e-6

SUBLANES = 8
LANES = 128
ROW_TILE_ROWS = D_MODEL // LANES
HALO = max(POOL_WINDOWS)
VMEM_LIMIT_BYTES = 58 * 1024 * 1024
EXPERT_CHUNKS = 4
EXPERT_TILE = 256
EXPERT_BLOCK_TOKENS = 4096
STAGE_PITCH = EXPERT_TILE + SUBLANES
FINAL_CHUNK = 256
ROW_BATCH = 8
WEIGHT_DMA_PARTS = 4
ROW_QUARTERS = 4

NEG_BIG = -1e30


def _layer_norm(x, g, b):
    mu = jnp.mean(x, axis=-1, keepdims=True)
    xc = x - mu
    var = jnp.mean(xc * xc, axis=-1, keepdims=True)
    return xc * lax.rsqrt(var + LN_EPS) * g + b


def _ln_kernel(x_ref, g_ref, b_ref, o_ref, *, rows):
    y = _layer_norm(x_ref[...], g_ref[...], b_ref[...])
    for j in range(ROW_TILE_ROWS):
        o_ref[pl.ds(j, rows, stride=ROW_TILE_ROWS), :] = y[:, j * LANES:(j + 1) * LANES]


def _ln_rows(x, g, b, rows):
    n, d = x.shape
    return pl.pallas_call(
        functools.partial(_ln_kernel, rows=rows),
        grid=(n // rows,),
        in_specs=[pl.BlockSpec((rows, d), lambda i: (i, 0)),
                  pl.BlockSpec((1, d), lambda i: (0, 0)),
                  pl.BlockSpec((1, d), lambda i: (0, 0))],
        out_specs=pl.BlockSpec((rows * ROW_TILE_ROWS, LANES), lambda i: (i, 0)),
        out_shape=jax.ShapeDtypeStruct((n * ROW_TILE_ROWS, LANES), F32),
        name="entry_ln",
    )(x, g.reshape(1, d), b.reshape(1, d))


def _mem_kv_kernel(m_ref, g_ref, b_ref, w_ref, o_ref):
    mn = _layer_norm(m_ref[0], g_ref[...], b_ref[...])
    o_ref[0] = jnp.dot(mn.astype(BF16), w_ref[...], preferred_element_type=F32).astype(BF16)


def _mem_kv(mem, g, b, w_all):
    bsz, m, d = mem.shape
    p = w_all.shape[1]
    return pl.pallas_call(
        _mem_kv_kernel,
        grid=(bsz,),
        in_specs=[pl.BlockSpec((1, m, d), lambda i: (i, 0, 0)),
                  pl.BlockSpec((1, d), lambda i: (0, 0)),
                  pl.BlockSpec((1, d), lambda i: (0, 0)),
                  pl.BlockSpec((d, p), lambda i: (0, 0))],
        out_specs=pl.BlockSpec((1, m, p), lambda i: (i, 0, 0)),
        out_shape=jax.ShapeDtypeStruct((bsz, m, p), BF16),
        name="mem_kv",
    )(mem, g.reshape(1, d), b.reshape(1, d), w_all)


def _mixer_kernel(x_ref, kv_ref, win_ref, convw_ref, poolw_ref, pscale_ref, glng_ref, glnb_ref,
                  gw_ref, gbias_ref, gng_ref, wout_ref, ln1g_ref, ln1b_ref, wrh_ref, wrl_ref, br_ref,
                  h1rt_ref, key_ref, gate_ref, cnt_ref, ext_ref, *, ts, key_shift, block_shift):
    s = pl.program_id(1)
    dg = D_GROUP

    x = jnp.concatenate([x_ref[pl.ds(j, ts, stride=ROW_TILE_ROWS), :] for j in range(ROW_TILE_ROWS)], axis=1)
    proj = jnp.dot(x.astype(BF16), win_ref[...], preferred_element_type=F32)
    gate_b = proj[:, 0:dg]
    gate_c = proj[:, dg:2 * dg]
    conv_in = proj[:, 2 * dg:3 * dg]
    pool_in = proj[:, 3 * dg:4 * dg]
    gmlp_in = proj[:, 4 * dg:6 * dg]
    mem_q = proj[:, 6 * dg:7 * dg]

    @pl.when(s == 0)
    def _():
        ext_ref[0:HALO, :] = jnp.zeros((HALO, 2 * dg), F32)

    z = gate_c * conv_in
    ext_ref[HALO:HALO + ts, 0:dg] = z
    ext_ref[HALO:HALO + ts, dg:2 * dg] = pool_in

    zm1 = ext_ref[pl.ds(HALO - 1, ts), 0:dg]
    zm2 = ext_ref[pl.ds(HALO - 2, ts), 0:dg]
    cw = convw_ref[...]
    y_conv = gate_b * (cw[0:1, :] * zm2 + cw[1:2, :] * zm1 + cw[2:3, :] * z)

    pos = s * ts + lax.broadcasted_iota(jnp.int32, (ts, 1), 0)
    half = dg // 2
    lane_h = lax.broadcasted_iota(jnp.int32, (ts, half), 1)
    diffs = []
    for hf in range(2):
        c0 = dg + hf * half
        w_lo, w_hi = POOL_WINDOWS[2 * hf], POOL_WINDOWS[2 * hf + 1]
        cur = ext_ref[pl.ds(HALO, ts), c0:c0 + half]
        acc = cur
        for k in range(1, w_lo):
            acc = acc + ext_ref[pl.ds(HALO - k, ts), c0:c0 + half]
        s_lo = acc
        for k in range(w_lo, w_hi):
            acc = acc + ext_ref[pl.ds(HALO - k, ts), c0:c0 + half]
        s_hi = acc
        cnt_lo = jnp.minimum(pos + 1, w_lo).astype(F32)
        cnt_hi = jnp.minimum(pos + 1, w_hi).astype(F32)
        mean = jnp.where(lane_h < POOL_CH, s_lo / cnt_lo, s_hi / cnt_hi)
        diffs.append(mean - cur)
    dpool = jnp.concatenate(diffs, axis=1)
    y_pool = jnp.dot(dpool.astype(BF16), poolw_ref[...], preferred_element_type=F32) * pscale_ref[...]

    ext_ref[0:HALO, :] = ext_ref[ts:ts + HALO, :]

    zg = jax.nn.gelu(gmlp_in, approximate=True)
    u = zg[:, 0:dg]
    v = _layer_norm(zg[:, dg:2 * dg], glng_ref[...], glnb_ref[...]).astype(BF16)
    lane_g = lax.broadcasted_iota(jnp.int32, (GMLP_BLOCK, dg), 1)
    gw = gw_ref[...]
    gbias = gbias_ref[...]
    gates = []
    for nb in range(ts // GMLP_BLOCK):
        vb = v[nb * GMLP_BLOCK:(nb + 1) * GMLP_BLOCK, :]
        r = jnp.dot(gw, vb, preferred_element_type=F32)
        g = r[0:GMLP_BLOCK, :]
        for h in range(1, GMLP_HEADS):
            g = jnp.where(lane_g >= h * GMLP_HEAD_DIM, r[h * GMLP_BLOCK:(h + 1) * GMLP_BLOCK, :], g)
        gates.append(g + gbias)
    y_gmlp = u * jnp.concatenate(gates, axis=0)

    kv = kv_ref[0]
    mk = kv[:, 0:dg]
    mv = kv[:, dg:2 * dg]
    lane_q = lax.broadcasted_iota(jnp.int32, (ts, dg), 1)
    qs = mem_q * (MEM_HEAD_DIM ** -0.5)
    y_mem = jnp.zeros((ts, dg), F32)
    for h in range(MEM_HEADS):
        in_head = (lane_q >= h * MEM_HEAD_DIM) & (lane_q < (h + 1) * MEM_HEAD_DIM)
        qh = jnp.where(in_head, qs, 0.0).astype(BF16)
        sc = lax.dot_general(qh, mk, (((1,), (1,)), ((), ())), preferred_element_type=F32)
        e = jnp.exp(sc - jnp.max(sc, axis=-1, keepdims=True))
        den = jnp.sum(e, axis=-1, keepdims=True)
        oh = jnp.dot(e.astype(BF16), mv, preferred_element_type=F32) / den
        y_mem = jnp.where(in_head, oh, y_mem)

    gng = gng_ref[...]
    groups = []
    for gi, y in enumerate((y_conv, y_pool, y_gmlp, y_mem)):
        ms = jnp.mean(y * y, axis=-1, keepdims=True)
        groups.append(y * lax.rsqrt(ms + RMS_EPS) * gng[:, gi * dg:(gi + 1) * dg])
    cat = jnp.concatenate(groups, axis=1).astype(BF16)
    mix = jnp.dot(cat, wout_ref[...], preferred_element_type=F32)

    h1 = _layer_norm(DEEPNORM_ALPHA * x + mix, ln1g_ref[...], ln1b_ref[...])
    for j in range(ROW_TILE_ROWS):
        h1rt_ref[pl.ds(j, ts, stride=ROW_TILE_ROWS), :] = h1[:, j * LANES:(j + 1) * LANES]

    h1_hi = h1.astype(BF16)
    h1_lo = (h1 - h1_hi.astype(F32)).astype(BF16)
    both = jnp.dot(h1_hi, wrl_ref[...], preferred_element_type=F32)
    logits = (both + pltpu.roll(both, LANES - N_EXPERTS, axis=1)
              + jnp.dot(h1_lo, wrh_ref[...], preferred_element_type=F32)) + br_ref[...]
    lane = lax.broadcasted_iota(jnp.int32, (ts, LANES), 1)
    lane_f = lane.astype(F32)
    vals, idxs = [], []
    cur_l = logits
    for _ in range(TOP_K):
        m = jnp.max(cur_l, axis=-1, keepdims=True)
        i_f = jnp.min(jnp.where(cur_l == m, lane_f, float(LANES)), axis=-1, keepdims=True)
        vals.append(m)
        idxs.append(i_f)
        cur_l = jnp.where(lane_f == i_f, -jnp.inf, cur_l)
    es = [jnp.exp(vk - vals[0]) for vk in vals]
    den = es[0] + es[1] + es[2] + es[3]
    idx_out = jnp.zeros((ts, LANES), F32)
    gate_out = jnp.zeros((ts, LANES), F32)
    hits = jnp.zeros((ts, LANES), F32)
    for k in range(TOP_K):
        idx_out = jnp.where(lane == k, idxs[k], idx_out)
        gate_out = jnp.where(lane == k, es[k] / den, gate_out)
        hits = hits + jnp.where(lane_f == idxs[k], 1.0, 0.0)
    tok = (pl.program_id(0) * pl.num_programs(1) + s) * ts + lax.broadcasted_iota(jnp.int32, (ts, LANES), 0)
    group = (tok >> block_shift) * GROUP_SLOTS + idx_out.astype(jnp.int32)
    key_ref[...] = jnp.where(lane < TOP_K, group * (1 << key_shift) + tok * TOP_K + lane, 0)
    gate_ref[...] = gate_out
    cnt_ref[0] = jnp.sum(hits, axis=0, keepdims=True)


def _mixer(h, kv_all, layer, lw, *, bsz, seq, ts, key_shift, block_shift):
    n = bsz * seq
    st = seq // ts
    dg = D_GROUP
    const = lambda b, s: (0, 0)
    kern = functools.partial(_mixer_kernel, ts=ts, key_shift=key_shift, block_shift=block_shift)
    return pl.pallas_call(
        kern,
        grid=(bsz, st),
        in_specs=[
            pl.BlockSpec((ts * ROW_TILE_ROWS, LANES), lambda b, s: (b * st + s, 0)),
            pl.BlockSpec((1, kv_all.shape[1], 2 * dg), lambda b, s: (b, 0, layer)),
            pl.BlockSpec((D_MODEL, D_IN_PROJ), const),
            pl.BlockSpec((3, dg), const),
            pl.BlockSpec((dg, dg), const),
            pl.BlockSpec((1, dg), const),
            pl.BlockSpec((1, dg), const),
            pl.BlockSpec((1, dg), const),
            pl.BlockSpec((GMLP_HEADS * GMLP_BLOCK, GMLP_BLOCK), const),
            pl.BlockSpec((GMLP_BLOCK, dg), const),
            pl.BlockSpec((1, D_MODEL), const),
            pl.BlockSpec((D_MODEL, D_MODEL), const),
            pl.BlockSpec((1, D_MODEL), const),
            pl.BlockSpec((1, D_MODEL), const),
            pl.BlockSpec((D_MODEL, LANES), const),
            pl.BlockSpec((D_MODEL, LANES), const),
            pl.BlockSpec((1, LANES), const),
        ],
        out_specs=[
            pl.BlockSpec((ts * ROW_TILE_ROWS, LANES), lambda b, s: (b * st + s, 0)),
            pl.BlockSpec((ts, LANES), lambda b, s: (b * st + s, 0)),
            pl.BlockSpec((ts, LANES), lambda b, s: (b * st + s, 0)),
            pl.BlockSpec((1, 1, LANES), lambda b, s: (b * st + s, 0, 0)),
        ],
        out_shape=[
            jax.ShapeDtypeStruct((n * ROW_TILE_ROWS, LANES), F32),
            jax.ShapeDtypeStruct((n, LANES), jnp.int32),
            jax.ShapeDtypeStruct((n, LANES), F32),
            jax.ShapeDtypeStruct((bsz * st, 1, LANES), F32),
        ],
        scratch_shapes=[pltpu.VMEM((ts + HALO, 2 * dg), F32)],
        compiler_params=pltpu.CompilerParams(
            dimension_semantics=("arbitrary", "arbitrary"), vmem_limit_bytes=VMEM_LIMIT_BYTES),
        name="mixer",
    )(h, kv_all, *lw)


def _expert_kernel(te_ref, nact_ref, winfo_ref, slot_ref, gate_ref,
                   h_hbm, w1_hbm, b1_ref, w2_hbm, b2_ref, g_ref, bt_ref, o_hbm,
                   xblk, ybuf, sin, sout, hbuf, w1buf, w2buf, fstage, xsem, osem, wsem, fsem,
                   *, tm, tbt, tpb, layer, tiled_out):
    i = pl.program_id(0)
    b = i // tpb
    tj = i - b * tpb
    n_act = nact_ref[b]
    par = lax.rem(i, 2)
    rt = ROW_TILE_ROWS
    blk_rows = tbt * rt
    sp = STAGE_PITCH
    cw = D_FF // EXPERT_CHUNKS
    base = (tj + 1) * tm

    def zero_row(tiles, width):
        bits = pltpu.bitcast(tiles[0], jnp.uint32)
        for t in tiles[1:]:
            bits = bits | pltpu.bitcast(t, jnp.uint32)
        z = ((bits >> 16) >> 16).astype(F32)[0:1, :]
        return jnp.concatenate([z] * (width // LANES), axis=1)

    def gather(first_entry, dst, lo=0, hi=tm):
        tiles = []
        entries = slot_ref.at[0, 0, pl.ds(first_entry, tm)]
        for r in range(lo, hi):
            off = pl.multiple_of((entries[r] >> 2) * rt, rt)
            tiles.append(xblk[pl.ds(off, rt), :])
            dst[pl.ds(r * rt, rt), :] = tiles[-1]
        return tiles

    def accumulate(first_entry, src, lo=0, hi=tm):
        sums = []
        entries = slot_ref.at[0, 0, pl.ds(first_entry, tm)]
        for r0 in range(lo, hi, ROW_BATCH):
            rs = range(r0, r0 + ROW_BATCH)
            slots = [entries[r] for r in rs]
            offs = [pl.multiple_of((s >> 2) * rt, rt) for s in slots]
            accs = [ybuf[pl.ds(o, rt), :] for o in offs]
            vals = [src[pl.ds(r, rt, stride=sp), :] for r in rs]
            for k, r in enumerate(rs):
                gate = gate_ref[0, 0, slots[k]]
                sums.append(accs[k] + gate * vals[k])
                ybuf[pl.ds(offs[k], rt), :] = sums[-1]
        return sums

    @pl.when(i == 0)
    def _():
        xblk[pl.ds(blk_rows, rt), :] = jnp.zeros((rt, LANES), F32)
        sout[...] = jnp.zeros(sout.shape, F32)

    def weight_copies(e, wslot):
        cps = []
        for src, dst in ((w1_hbm, w1buf), (w2_hbm, w2buf)):
            rows_p = src.shape[2] // WEIGHT_DMA_PARTS
            for p in range(WEIGHT_DMA_PARTS):
                sl = pl.ds(p * rows_p, rows_p)
                cps.append((pltpu.make_async_copy(src.at[layer, e, sl], dst.at[wslot, sl], wsem.at[wslot]), p % 2))
        return cps

    @pl.when(tj < n_act)
    def _():
        first = tj == 0
        last = tj == n_act - 1
        row0 = pl.multiple_of(b * blk_rows, blk_rows)
        info = winfo_ref[i]
        wpar = (info >> 2) & 1

        @pl.when(i == 0)
        def _():
            for cp, prio in weight_copies(te_ref[0], 0):
                cp.start(priority=prio)

        @pl.when((info & 1) == 1)
        def _():
            for cp, _ in weight_copies(te_ref[i], wpar):
                cp.wait()

            @pl.when((info & 2) == 2)
            def _():
                for cp, prio in weight_copies(info >> 5, 1 - wpar):
                    cp.start(priority=prio)

        w1_ref = w1buf.at[wpar]
        w2_ref = w2buf.at[wpar]

        @pl.when(first)
        def _():
            load = pltpu.make_async_copy(h_hbm.at[pl.ds(row0, blk_rows), :], xblk.at[pl.ds(0, blk_rows), :], xsem)
            load.start()
            zrows = FINAL_CHUNK * rt

            def zero(c, carry):
                ybuf[pl.ds(pl.multiple_of(c * zrows, zrows), zrows), :] = jnp.zeros((zrows, LANES), F32)
                return carry

            lax.fori_loop(0, tbt // FINAL_CHUNK, zero, 0)
            ybuf[pl.ds(blk_rows, rt), :] = jnp.zeros((rt, LANES), F32)
            load.wait()
            gather(base, sin.at[par])

        so = sout.at[par]

        def expert(m):
            xs = sin.at[par]
            x = jnp.concatenate([xs[pl.ds(j, m, stride=rt), :] for j in range(rt)], axis=1)
            xb = x.astype(BF16)
            per1 = tm // (2 * EXPERT_CHUNKS)
            per2 = tm // EXPERT_CHUNKS
            for c in range(2 * EXPERT_CHUNKS):
                tie = zero_row(accumulate(base - tm, sout.at[1 - par], c * per1, (c + 1) * per1), cw)
                hbuf[0:m, c * cw:(c + 1) * cw] = (
                    jnp.dot(xb, w1_ref[:, c * cw:(c + 1) * cw], preferred_element_type=F32)
                    + (b1_ref[:, c * cw:(c + 1) * cw] + tie))
            out = None
            for c in range(EXPERT_CHUNKS):
                tie = zero_row(gather(base + tm, sin.at[1 - par], c * per2, (c + 1) * per2), cw)
                glu = jnp.minimum(hbuf[0:m, c * cw:(c + 1) * cw], SWIGLU_LIMIT)
                lin = jnp.clip(hbuf[0:m, D_FF + c * cw:D_FF + (c + 1) * cw] + tie, -SWIGLU_LIMIT, SWIGLU_LIMIT)
                act = glu * jax.nn.sigmoid(SWIGLU_ALPHA * glu) * (lin + 1.0)
                part = jnp.dot(act.astype(BF16), w2_ref[c * cw:(c + 1) * cw, :], preferred_element_type=F32)
                out = part if out is None else out + part
            out = out + b2_ref[...]
            for j in range(rt):
                so[pl.ds(j * sp, m), :] = out[:, j * LANES:(j + 1) * LANES]

        quarters = (info >> 3) & 3
        for q in range(ROW_QUARTERS):
            pl.when(quarters == q)(functools.partial(expert, (q + 1) * tm // ROW_QUARTERS))

        @pl.when(last)
        def _():
            accumulate(base, so)
            crows = FINAL_CHUNK * rt

            def finish(c, carry):
                c0 = pl.multiple_of(c * crows, crows)
                r = jnp.concatenate(
                    [DEEPNORM_ALPHA * xblk[pl.ds(c0 + jj, FINAL_CHUNK, stride=rt), :]
                     + ybuf[pl.ds(c0 + jj, FINAL_CHUNK, stride=rt), :] for jj in range(rt)], axis=1)
                y = _layer_norm(r, g_ref[...], bt_ref[...])
                if tiled_out:
                    for jj in range(rt):
                        ybuf[pl.ds(c0 + jj, FINAL_CHUNK, stride=rt), :] = y[:, jj * LANES:(jj + 1) * LANES]
                    pltpu.make_async_copy(
                        ybuf.at[pl.ds(c0, crows), :],
                        o_hbm.at[pl.ds(pl.multiple_of(row0 + c * crows, crows), crows), :], osem).start()
                else:
                    p = lax.rem(c, 2)

                    @pl.when(c >= 2)
                    def _():
                        pltpu.make_async_copy(fstage.at[p], o_hbm.at[pl.ds(0, FINAL_CHUNK), :], fsem.at[p]).wait()

                    fstage[p] = y
                    tok0 = pl.multiple_of(b * tbt + c * FINAL_CHUNK, FINAL_CHUNK)
                    pltpu.make_async_copy(fstage.at[p], o_hbm.at[pl.ds(tok0, FINAL_CHUNK), :], fsem.at[p]).start()
                return carry

            lax.fori_loop(0, tbt // FINAL_CHUNK, finish, 0)
            if tiled_out:
                pltpu.make_async_copy(ybuf.at[pl.ds(0, blk_rows), :], o_hbm.at[pl.ds(row0, blk_rows), :], osem).wait()
            else:
                for p in range(min(2, tbt // FINAL_CHUNK)):
                    pltpu.make_async_copy(fstage.at[p], o_hbm.at[pl.ds(0, FINAL_CHUNK), :], fsem.at[p]).wait()


def _experts(h1rt, te, nact, winfo, slots_blk, gates_blk, w1b, b1, w2b, b2, ln_g, ln_b, layer, *, tm, tbt, tiled_out):
    nt = te.shape[0]
    tpb = nt // nact.shape[0]
    n_rows = h1rt.shape[0]
    out_shape = (n_rows, LANES) if tiled_out else (n_rows // ROW_TILE_ROWS, D_MODEL)
    kern = functools.partial(_expert_kernel, tm=tm, tbt=tbt, tpb=tpb, layer=layer, tiled_out=tiled_out)
    per_block = lambda i, te, na, wi: (i // tpb, 0, 0)
    per_expert = lambda i, te, na, wi: (layer, te[i], 0, 0)
    const = lambda i, te, na, wi: (0, 0)
    grid_spec = pltpu.PrefetchScalarGridSpec(
        num_scalar_prefetch=3,
        grid=(nt,),
        in_specs=[
            pl.BlockSpec((1, 1, slots_blk.shape[2]), per_block, memory_space=pltpu.SMEM,
                         pipeline_mode=pl.Buffered(1)),
            pl.BlockSpec((1, 1, gates_blk.shape[2]), per_block, memory_space=pltpu.SMEM,
                         pipeline_mode=pl.Buffered(1)),
            pl.BlockSpec(memory_space=pl.ANY),
            pl.BlockSpec(memory_space=pl.ANY),
            pl.BlockSpec((None, None, 1, 2 * D_FF), per_expert),
            pl.BlockSpec(memory_space=pl.ANY),
            pl.BlockSpec((None, None, 1, D_MODEL), per_expert),
            pl.BlockSpec((1, D_MODEL), const),
            pl.BlockSpec((1, D_MODEL), const),
        ],
        out_specs=pl.BlockSpec(memory_space=pl.ANY),
        scratch_shapes=[
            pltpu.VMEM(((tbt + 1) * ROW_TILE_ROWS, LANES), F32),
            pltpu.VMEM(((tbt + 1) * ROW_TILE_ROWS, LANES), F32),
            pltpu.VMEM((2, tm * ROW_TILE_ROWS, LANES), F32),
            pltpu.VMEM((2, ROW_TILE_ROWS * STAGE_PITCH, LANES), F32),
            pltpu.VMEM((tm, 2 * D_FF), F32),
            pltpu.VMEM((2, D_MODEL, 2 * D_FF), BF16),
            pltpu.VMEM((2, D_FF, D_MODEL), BF16),
            pltpu.VMEM((2, FINAL_CHUNK, D_MODEL), F32),
            pltpu.SemaphoreType.DMA,
            pltpu.SemaphoreType.DMA,
            pltpu.SemaphoreType.DMA((2,)),
            pltpu.SemaphoreType.DMA((2,)),
        ],
    )
    return pl.pallas_call(
        kern,
        grid_spec=grid_spec,
        out_shape=jax.ShapeDtypeStruct(out_shape, F32),
        compiler_params=pltpu.CompilerParams(
            dimension_semantics=("arbitrary",), vmem_limit_bytes=VMEM_LIMIT_BYTES),
        name="experts",
    )(te, nact, winfo, slots_blk, gates_blk, h1rt, w1b, b1, w2b, b2, ln_g, ln_b)


def _key_shift(n, tm):
    return max((n * TOP_K - 1).bit_length(), (N_EXPERTS * tm - 1).bit_length()) + 1


def _routing_plan(keys, counts, n, tm, tbt):
    nb = n // tbt
    tpb = tbt * TOP_K // tm + N_EXPERTS
    shift = _key_shift(n, tm)
    flag = 1 << (shift - 1)
    need = (-counts) % tm
    b_ids = jnp.arange(nb, dtype=jnp.int32)[:, None, None]
    e_ids = jnp.arange(N_EXPERTS, dtype=jnp.int32)[None, :, None]
    j_ids = jnp.arange(tm, dtype=jnp.int32)[None, None, :]
    pad_keys = jnp.where(j_ids < need[:, :, None],
                         (b_ids * GROUP_SLOTS + e_ids) * (1 << shift) + flag + j_ids,
                         (b_ids * GROUP_SLOTS + N_EXPERTS) * (1 << shift) + flag + e_ids * tm + j_ids)
    ks = jnp.sort(jnp.concatenate([keys.reshape(nb, tbt * TOP_K), pad_keys.reshape(nb, N_EXPERTS * tm)], axis=1),
                  axis=1)
    low = ks & (flag - 1)
    real = (ks & flag) == 0
    slot = jnp.where(real, low & (tbt * TOP_K - 1), tbt * TOP_K)
    edge = jnp.full((nb, tm), tbt * TOP_K, jnp.int32)
    slots_blk = jnp.concatenate([edge, slot, edge], axis=1).reshape(nb, 1, (tpb + 2) * tm)
    e_t = (ks[:, ::tm] >> shift) % GROUP_SLOTS
    active = e_t < N_EXPERTS
    nact = jnp.sum(active.astype(jnp.int32), axis=1)
    active = active.reshape(-1)
    te = jnp.where(active, e_t.reshape(-1), N_EXPERTS - 1)
    nt = nb * tpb
    tile = jnp.arange(nt, dtype=jnp.int32)
    prev_same = (tile % tpb != 0) & (jnp.roll(te, 1) == te) & jnp.roll(active, 1)
    gstart = active & ~prev_same
    gidx = jnp.cumsum(gstart.astype(jnp.int32)) - 1
    starts = jnp.where(gstart, tile, nt)
    nxt = jnp.flip(lax.cummin(jnp.flip(jnp.roll(starts, -1).at[-1].set(nt))))
    has_next = nxt < nt
    tnext = te[jnp.minimum(nxt, nt - 1)]
    n_real = jnp.sum(real.reshape(nt, tm).astype(jnp.int32), axis=1)
    quarters = jnp.clip((n_real + tm // ROW_QUARTERS - 1) // (tm // ROW_QUARTERS), 1, ROW_QUARTERS) - 1
    winfo = (gstart.astype(jnp.int32) | (has_next.astype(jnp.int32) << 1) | ((gidx & 1) << 2)
             | (quarters << 3) | (tnext << 5))
    return te, nact, winfo, slots_blk


def _pick_tile(total, want):
    t = min(total, want)
    while total % t:
        t //= 2
    return t


def kernel(x, mem, ln_in_g, ln_in_b, mem_ln_g, mem_ln_b, w_in, conv_w, pool_w, pool_scale, gmlp_ln_g, gmlp_ln_b,
           gmlp_w, gmlp_b, w_mem_kv, group_norm_g, w_out, ln1_g, ln1_b, w_router, b_router, w1, b1, w2, b2,
           ln2_g, ln2_b):
    bsz, seq, d = x.shape
    depth = w_in.shape[0]
    n = bsz * seq
    dg = D_GROUP
    ts = _pick_tile(seq, 512)
    tm = EXPERT_TILE
    tbt = _pick_tile(n, EXPERT_BLOCK_TOKENS)
    nb = n // tbt
    assert N_EXPERTS == 32 and tbt % ts == 0 and tbt & (tbt - 1) == 0 and tbt % FINAL_CHUNK == 0
    key_shift = _key_shift(n, tm)
    block_shift = tbt.bit_length() - 1

    h = _ln_rows(x.reshape(n, d), ln_in_g, ln_in_b, _pick_tile(n, 1024))

    wkv_all = jnp.transpose(w_mem_kv, (1, 0, 2)).reshape(d, depth * 2 * dg).astype(BF16)
    kv_all = _mem_kv(mem, mem_ln_g, mem_ln_b, wkv_all)

    pos = jnp.arange(GMLP_BLOCK)
    mask = (pos[None, :] // CHUNK) <= (pos[:, None] // CHUNK)
    eye = jnp.eye(len(POOL_WINDOWS), dtype=F32)

    wr_pad = jnp.zeros((depth, d, LANES), F32).at[:, :, :N_EXPERTS].set(w_router)
    wr_hi = wr_pad.astype(BF16)
    wr_lo = wr_hi + jnp.roll((wr_pad - wr_hi.astype(F32)).astype(BF16), N_EXPERTS, axis=2)
    br_pad = jnp.full((depth, 1, LANES), NEG_BIG, F32).at[:, 0, :N_EXPERTS].set(b_router)
    w1b = w1.astype(BF16)
    w2b = w2.astype(BF16)
    b1r = b1.reshape(depth, N_EXPERTS, 1, 2 * D_FF)
    b2r = b2.reshape(depth, N_EXPERTS, 1, d)

    for l in range(depth):
        pool_bd = jnp.einsum('gh,gcd->gchd', eye, pool_w[l]).reshape(dg, dg).astype(BF16)
        gw = jnp.where(mask[None], gmlp_w[l], 0.0).reshape(GMLP_HEADS * GMLP_BLOCK, GMLP_BLOCK).astype(BF16)
        gbias = jnp.repeat(gmlp_b[l].T, GMLP_HEAD_DIM, axis=1)
        lw = (w_in[l].astype(BF16), conv_w[l], pool_bd, pool_scale[l].reshape(1, dg),
              gmlp_ln_g[l].reshape(1, dg), gmlp_ln_b[l].reshape(1, dg), gw, gbias,
              group_norm_g[l].reshape(1, d), w_out[l].astype(BF16),
              ln1_g[l].reshape(1, d), ln1_b[l].reshape(1, d), wr_hi[l], wr_lo[l], br_pad[l])
        h1rt, keys, gates, cnt = _mixer(h, kv_all, l, lw, bsz=bsz, seq=seq, ts=ts,
                                        key_shift=key_shift, block_shift=block_shift)
        counts = jnp.sum(cnt.reshape(nb, tbt // ts, LANES), axis=1)[:, :N_EXPERTS].astype(jnp.int32)
        te, nact, winfo, slots_blk = _routing_plan(keys[:, :TOP_K].reshape(-1), counts, n, tm, tbt)
        gates_blk = jnp.pad(gates[:, :TOP_K].reshape(nb, 1, tbt * TOP_K), ((0, 0), (0, 0), (0, LANES)))
        h = _experts(h1rt, te, nact, winfo, slots_blk, gates_blk, w1b, b1r, w2b, b2r,
                     ln2_g[l].reshape(1, d), ln2_b[l].reshape(1, d), l,
                     tm=tm, tbt=tbt, tiled_out=l + 1 < depth)
    return h.reshape(bsz, seq, d)
```

```python
import functools

import jax
import jax.numpy as jnp
from jax import lax
from jax.experimental import pallas as pl
from jax.experimental.pallas import tpu as pltpu

F32 = jnp.float32
BF16 = jnp.bfloat16

D_MODEL = 1024
N_GROUPS = 4
D_GROUP = D_MODEL // N_GROUPS
D_IN_PROJ = 7 * D_GROUP
POOL_WINDOWS = (2, 4, 8, 16)
POOL_CH = D_GROUP // len(POOL_WINDOWS)
GMLP_BLOCK = 128
GMLP_HEADS = 4
GMLP_HEAD_DIM = D_GROUP // GMLP_HEADS
CHUNK = 64
MEM_HEADS = 4
MEM_HEAD_DIM = D_GROUP // MEM_HEADS
N_EXPERTS = 32
GROUP_SLOTS = N_EXPERTS + 1
TOP_K = 4
D_FF = D_MODEL
SWIGLU_ALPHA = 1.702
SWIGLU_LIMIT = 7.0
REF_DEPTH = 4
DEEPNORM_ALPHA = (2 * REF_DEPTH) ** 0.25
LN_EPS = 1e-5
RMS_EPS = 1e-6

SUBLANES = 8
LANES = 128
ROW_TILE_ROWS = D_MODEL // LANES
HALO = max(POOL_WINDOWS)
VMEM_LIMIT_BYTES = 58 * 1024 * 1024
EXPERT_CHUNKS = 4
EXPERT_TILE = 256
EXPERT_BLOCK_TOKENS = 4096
STAGE_PITCH = EXPERT_TILE + SUBLANES
FINAL_CHUNK = 256
ROW_BATCH = 8
WEIGHT_DMA_PARTS = 4
ROW_QUARTERS = 4

NEG_BIG = -1e30


def _layer_norm(x, g, b):
    mu = jnp.mean(x, axis=-1, keepdims=True)
    xc = x - mu
    var = jnp.mean(xc * xc, axis=-1, keepdims=True)
    return xc * lax.rsqrt(var + LN_EPS) * g + b


def _ln_kernel(x_ref, g_ref, b_ref, o_ref, *, rows):
    y = _layer_norm(x_ref[...], g_ref[...], b_ref[...])
    for j in range(ROW_TILE_ROWS):
        o_ref[pl.ds(j, rows, stride=ROW_TILE_ROWS), :] = y[:, j * LANES:(j + 1) * LANES]


def _ln_rows(x, g, b, rows):
    n, d = x.shape
    return pl.pallas_call(
        functools.partial(_ln_kernel, rows=rows),
        grid=(n // rows,),
        in_specs=[pl.BlockSpec((rows, d), lambda i: (i, 0)),
                  pl.BlockSpec((1, d), lambda i: (0, 0)),
                  pl.BlockSpec((1, d), lambda i: (0, 0))],
        out_specs=pl.BlockSpec((rows * ROW_TILE_ROWS, LANES), lambda i: (i, 0)),
        out_shape=jax.ShapeDtypeStruct((n * ROW_TILE_ROWS, LANES), F32),
        name="entry_ln",
    )(x, g.reshape(1, d), b.reshape(1, d))


def _mem_kv_kernel(m_ref, g_ref, b_ref, w_ref, o_ref):
    mn = _layer_norm(m_ref[0], g_ref[...], b_ref[...])
    o_ref[0] = jnp.dot(mn.astype(BF16), w_ref[...], preferred_element_type=F32).astype(BF16)


def _mem_kv(mem, g, b, w_all):
    bsz, m, d = mem.shape
    p = w_all.shape[1]
    return pl.pallas_call(
        _mem_kv_kernel,
        grid=(bsz,),
        in_specs=[pl.BlockSpec((1, m, d), lambda i: (i, 0, 0)),
                  pl.BlockSpec((1, d), lambda i: (0, 0)),
                  pl.BlockSpec((1, d), lambda i: (0, 0)),
                  pl.BlockSpec((d, p), lambda i: (0, 0))],
        out_specs=pl.BlockSpec((1, m, p), lambda i: (i, 0, 0)),
        out_shape=jax.ShapeDtypeStruct((bsz, m, p), BF16),
        name="mem_kv",
    )(mem, g.reshape(1, d), b.reshape(1, d), w_all)


def _mixer_kernel(x_ref, kv_ref, win_ref, convw_ref, poolw_ref, pscale_ref, glng_ref, glnb_ref,
                  gw_ref, gbias_ref, gng_ref, wout_ref, ln1g_ref, ln1b_ref, wrh_ref, wrl_ref, br_ref,
                  h1rt_ref, key_ref, gate_ref, cnt_ref, ext_ref, *, ts, key_shift, block_shift):
    s = pl.program_id(1)
    dg = D_GROUP

    x = jnp.concatenate([x_ref[pl.ds(j, ts, stride=ROW_TILE_ROWS), :] for j in range(ROW_TILE_ROWS)], axis=1)
    proj = jnp.dot(x.astype(BF16), win_ref[...], preferred_element_type=F32)
    gate_b = proj[:, 0:dg]
    gate_c = proj[:, dg:2 * dg]
    conv_in = proj[:, 2 * dg:3 * dg]
    pool_in = proj[:, 3 * dg:4 * dg]
    gmlp_in = proj[:, 4 * dg:6 * dg]
    mem_q = proj[:, 6 * dg:7 * dg]

    @pl.when(s == 0)
    def _():
        ext_ref[0:HALO, :] = jnp.zeros((HALO, 2 * dg), F32)

    z = gate_c * conv_in
    ext_ref[HALO:HALO + ts, 0:dg] = z
    ext_ref[HALO:HALO + ts, dg:2 * dg] = pool_in

    zm1 = ext_ref[pl.ds(HALO - 1, ts), 0:dg]
    zm2 = ext_ref[pl.ds(HALO - 2, ts), 0:dg]
    cw = convw_ref[...]
    y_conv = gate_b * (cw[0:1, :] * zm2 + cw[1:2, :] * zm1 + cw[2:3, :] * z)

    pos = s * ts + lax.broadcasted_iota(jnp.int32, (ts, 1), 0)
    half = dg // 2
    lane_h = lax.broadcasted_iota(jnp.int32, (ts, half), 1)
    diffs = []
    for hf in range(2):
        c0 = dg + hf * half
        w_lo, w_hi = POOL_WINDOWS[2 * hf], POOL_WINDOWS[2 * hf + 1]
        cur = ext_ref[pl.ds(HALO, ts), c0:c0 + half]
        acc = cur
        for k in range(1, w_lo):
            acc = acc + ext_ref[pl.ds(HALO - k, ts), c0:c0 + half]
        s_lo = acc
        for k in range(w_lo, w_hi):
            acc = acc + ext_ref[pl.ds(HALO - k, ts), c0:c0 + half]
        s_hi = acc
        cnt_lo = jnp.minimum(pos + 1, w_lo).astype(F32)
        cnt_hi = jnp.minimum(pos + 1, w_hi).astype(F32)
        mean = jnp.where(lane_h < POOL_CH, s_lo / cnt_lo, s_hi / cnt_hi)
        diffs.append(mean - cur)
    dpool = jnp.concatenate(diffs, axis=1)
    y_pool = jnp.dot(dpool.astype(BF16), poolw_ref[...], preferred_element_type=F32) * pscale_ref[...]

    ext_ref[0:HALO, :] = ext_ref[ts:ts + HALO, :]

    zg = jax.nn.gelu(gmlp_in, approximate=True)
    u = zg[:, 0:dg]
    v = _layer_norm(zg[:, dg:2 * dg], glng_ref[...], glnb_ref[...]).astype(BF16)
    lane_g = lax.broadcasted_iota(jnp.int32, (GMLP_BLOCK, dg), 1)
    gw = gw_ref[...]
    gbias = gbias_ref[...]
    gates = []
    for nb in range(ts // GMLP_BLOCK):
        vb = v[nb * GMLP_BLOCK:(nb + 1) * GMLP_BLOCK, :]
        r = jnp.dot(gw, vb, preferred_element_type=F32)
        g = r[0:GMLP_BLOCK, :]
        for h in range(1, GMLP_HEADS):
            g = jnp.where(lane_g >= h * GMLP_HEAD_DIM, r[h * GMLP_BLOCK:(h + 1) * GMLP_BLOCK, :], g)
        gates.append(g + gbias)
    y_gmlp = u * jnp.concatenate(gates, axis=0)

    kv = kv_ref[0]
    mk = kv[:, 0:dg]
    mv = kv[:, dg:2 * dg]
    lane_q = lax.broadcasted_iota(jnp.int32, (ts, dg), 1)
    qs = mem_q * (MEM_HEAD_DIM ** -0.5)
    y_mem = jnp.zeros((ts, dg), F32)
    for h in range(MEM_HEADS):
        in_head = (lane_q >= h * MEM_HEAD_DIM) & (lane_q < (h + 1) * MEM_HEAD_DIM)
        qh = jnp.where(in_head, qs, 0.0).astype(BF16)
        sc = lax.dot_general(qh, mk, (((1,), (1,)), ((), ())), preferred_element_type=F32)
        e = jnp.exp(sc - jnp.max(sc, axis=-1, keepdims=True))
        den = jnp.sum(e, axis=-1, keepdims=True)
        oh = jnp.dot(e.astype(BF16), mv, preferred_element_type=F32) / den
        y_mem = jnp.where(in_head, oh, y_mem)

    gng = gng_ref[...]
    groups = []
    for gi, y in enumerate((y_conv, y_pool, y_gmlp, y_mem)):
        ms = jnp.mean(y * y, axis=-1, keepdims=True)
        groups.append(y * lax.rsqrt(ms + RMS_EPS) * gng[:, gi * dg:(gi + 1) * dg])
    cat = jnp.concatenate(groups, axis=1).astype(BF16)
    mix = jnp.dot(cat, wout_ref[...], preferred_element_type=F32)

    h1 = _layer_norm(DEEPNORM_ALPHA * x + mix, ln1g_ref[...], ln1b_ref[...])
    for j in range(ROW_TILE_ROWS):
        h1rt_ref[pl.ds(j, ts, stride=ROW_TILE_ROWS), :] = h1[:, j * LANES:(j + 1) * LANES]

    h1_hi = h1.astype(BF16)
    h1_lo = (h1 - h1_hi.astype(F32)).astype(BF16)
    both = jnp.dot(h1_hi, wrl_ref[...], preferred_element_type=F32)
    logits = (both + pltpu.roll(both, LANES - N_EXPERTS, axis=1)
              + jnp.dot(h1_lo, wrh_ref[...], preferred_element_type=F32)) + br_ref[...]
    lane = lax.broadcasted_iota(jnp.int32, (ts, LANES), 1)
    lane_f = lane.astype(F32)
    vals, idxs = [], []
    cur_l = logits
    for _ in range(TOP_K):
        m = jnp.max(cur_l, axis=-1, keepdims=True)
        i_f = jnp.min(jnp.where(cur_l == m, lane_f, float(LANES)), axis=-1, keepdims=True)
        vals.append(m)
        idxs.append(i_f)
        cur_l = jnp.where(lane_f == i_f, -jnp.inf, cur_l)
    es = [jnp.exp(vk - vals[0]) for vk in vals]
    den = es[0] + es[1] + es[2] + es[3]
    idx_out = jnp.zeros((ts, LANES), F32)
    gate_out = jnp.zeros((ts, LANES), F32)
    hits = jnp.zeros((ts, LANES), F32)
    for k in range(TOP_K):
        idx_out = jnp.where(lane == k, idxs[k], idx_out)
        gate_out = jnp.where(lane == k, es[k] / den, gate_out)
        hits = hits + jnp.where(lane_f == idxs[k], 1.0, 0.0)
    tok = (pl.program_id(0) * pl.num_programs(1) + s) * ts + lax.broadcasted_iota(jnp.int32, (ts, LANES), 0)
    group = (tok >> block_shift) * GROUP_SLOTS + idx_out.astype(jnp.int32)
    key_ref[...] = jnp.where(lane < TOP_K, group * (1 << key_shift) + tok * TOP_K + lane, 0)
    gate_ref[...] = gate_out
    cnt_ref[0] = jnp.sum(hits, axis=0, keepdims=True)


def _mixer(h, kv_all, layer, lw, *, bsz, seq, ts, key_shift, block_shift):
    n = bsz * seq
    st = seq // ts
    dg = D_GROUP
    const = lambda b, s: (0, 0)
    kern = functools.partial(_mixer_kernel, ts=ts, key_shift=key_shift, block_shift=block_shift)
    return pl.pallas_call(
        kern,
        grid=(bsz, st),
        in_specs=[
            pl.BlockSpec((ts * ROW_TILE_ROWS, LANES), lambda b, s: (b * st + s, 0)),
            pl.BlockSpec((1, kv_all.shape[1], 2 * dg), lambda b, s: (b, 0, layer)),
            pl.BlockSpec((D_MODEL, D_IN_PROJ), const),
            pl.BlockSpec((3, dg), const),
            pl.BlockSpec((dg, dg), const),
            pl.BlockSpec((1, dg), const),
            pl.BlockSpec((1, dg), const),
            pl.BlockSpec((1, dg), const),
            pl.BlockSpec((GMLP_HEADS * GMLP_BLOCK, GMLP_BLOCK), const),
            pl.BlockSpec((GMLP_BLOCK, dg), const),
            pl.BlockSpec((1, D_MODEL), const),
            pl.BlockSpec((D_MODEL, D_MODEL), const),
            pl.BlockSpec((1, D_MODEL), const),
            pl.BlockSpec((1, D_MODEL), const),
            pl.BlockSpec((D_MODEL, LANES), const),
            pl.BlockSpec((D_MODEL, LANES), const),
            pl.BlockSpec((1, LANES), const),
        ],
        out_specs=[
            pl.BlockSpec((ts * ROW_TILE_ROWS, LANES), lambda b, s: (b * st + s, 0)),
            pl.BlockSpec((ts, LANES), lambda b, s: (b * st + s, 0)),
            pl.BlockSpec((ts, LANES), lambda b, s: (b * st + s, 0)),
            pl.BlockSpec((1, 1, LANES), lambda b, s: (b * st + s, 0, 0)),
        ],
        out_shape=[
            jax.ShapeDtypeStruct((n * ROW_TILE_ROWS, LANES), F32),
            jax.ShapeDtypeStruct((n, LANES), jnp.int32),
            jax.ShapeDtypeStruct((n, LANES), F32),
            jax.ShapeDtypeStruct((bsz * st, 1, LANES), F32),
        ],
        scratch_shapes=[pltpu.VMEM((ts + HALO, 2 * dg), F32)],
        compiler_params=pltpu.CompilerParams(
            dimension_semantics=("arbitrary", "arbitrary"), vmem_limit_bytes=VMEM_LIMIT_BYTES),
        name="mixer",
    )(h, kv_all, *lw)


def _expert_kernel(te_ref, nact_ref, winfo_ref, slot_ref, gate_ref,
                   h_hbm, w1_hbm, b1_ref, w2_hbm, b2_ref, g_ref, bt_ref, o_hbm,
                   xblk, ybuf, sin, sout, hbuf, w1buf, w2buf, fstage, xsem, osem, wsem, fsem,
                   *, tm, tbt, tpb, layer, tiled_out):
    i = pl.program_id(0)
    b = i // tpb
    tj = i - b * tpb
    n_act = nact_ref[b]
    par = lax.rem(i, 2)
    rt = ROW_TILE_ROWS
    blk_rows = tbt * rt
    sp = STAGE_PITCH
    cw = D_FF // EXPERT_CHUNKS
    base = (tj + 1) * tm

    def zero_row(tiles, width):
        bits = pltpu.bitcast(tiles[0], jnp.uint32)
        for t in tiles[1:]:
            bits = bits | pltpu.bitcast(t, jnp.uint32)
        z = ((bits >> 16) >> 16).astype(F32)[0:1, :]
        return jnp.concatenate([z] * (width // LANES), axis=1)

    def gather(first_entry, dst, lo=0, hi=tm):
        tiles = []
        entries = slot_ref.at[0, 0, pl.ds(first_entry, tm)]
        for r in range(lo, hi):
            off = pl.multiple_of((entries[r] >> 2) * rt, rt)
            tiles.append(xblk[pl.ds(off, rt), :])
            dst[pl.ds(r * rt, rt), :] = tiles[-1]
        return tiles

    def accumulate(first_entry, src, lo=0, hi=tm):
        sums = []
        entries = slot_ref.at[0, 0, pl.ds(first_entry, tm)]
        for r0 in range(lo, hi, ROW_BATCH):
            rs = range(r0, r0 + ROW_BATCH)
            slots = [entries[r] for r in rs]
            offs = [pl.multiple_of((s >> 2) * rt, rt) for s in slots]
            accs = [ybuf[pl.ds(o, rt), :] for o in offs]
            vals = [src[pl.ds(r, rt, stride=sp), :] for r in rs]
            for k, r in enumerate(rs):
                gate = gate_ref[0, 0, slots[k]]
                sums.append(accs[k] + gate * vals[k])
                ybuf[pl.ds(offs[k], rt), :] = sums[-1]
        return sums

    @pl.when(i == 0)
    def _():
        xblk[pl.ds(blk_rows, rt), :] = jnp.zeros((rt, LANES), F32)
        sout[...] = jnp.zeros(sout.shape, F32)

    def weight_copies(e, wslot):
        cps = []
        for src, dst in ((w1_hbm, w1buf), (w2_hbm, w2buf)):
            rows_p = src.shape[2] // WEIGHT_DMA_PARTS
            for p in range(WEIGHT_DMA_PARTS):
                sl = pl.ds(p * rows_p, rows_p)
                cps.append((pltpu.make_async_copy(src.at[layer, e, sl], dst.at[wslot, sl], wsem.at[wslot]), p % 2))
        return cps

    @pl.when(tj < n_act)
    def _():
        first = tj == 0
        last = tj == n_act - 1
        row0 = pl.multiple_of(b * blk_rows, blk_rows)
        info = winfo_ref[i]
        wpar = (info >> 2) & 1

        @pl.when(i == 0)
        def _():
            for cp, prio in weight_copies(te_ref[0], 0):
                cp.start(priority=prio)

        @pl.when((info & 1) == 1)
        def _():
            for cp, _ in weight_copies(te_ref[i], wpar):
                cp.wait()

            @pl.when((info & 2) == 2)
            def _():
                for cp, prio in weight_copies(info >> 5, 1 - wpar):
                    cp.start(priority=prio)

        w1_ref = w1buf.at[wpar]
        w2_ref = w2buf.at[wpar]

        @pl.when(first)
        def _():
            load = pltpu.make_async_copy(h_hbm.at[pl.ds(row0, blk_rows), :], xblk.at[pl.ds(0, blk_rows), :], xsem)
            load.start()
            if tiled_out:
                @pl.when(b > 0)
                def _():
                    pltpu.make_async_copy(ybuf.at[pl.ds(0, blk_rows), :], o_hbm.at[pl.ds(0, blk_rows), :], osem).wait()
            zrows = FINAL_CHUNK * rt

            def zero(c, carry):
                ybuf[pl.ds(pl.multiple_of(c * zrows, zrows), zrows), :] = jnp.zeros((zrows, LANES), F32)
                return carry

            lax.fori_loop(0, tbt // FINAL_CHUNK, zero, 0)
            ybuf[pl.ds(blk_rows, rt), :] = jnp.zeros((rt, LANES), F32)
            load.wait()
            gather(base, sin.at[par])

        so = sout.at[par]

        def expert(m):
            xs = sin.at[par]
            x = jnp.concatenate([xs[pl.ds(j, m, stride=rt), :] for j in range(rt)], axis=1)
            xb = x.astype(BF16)
            per1 = tm // (2 * EXPERT_CHUNKS)
            per2 = tm // EXPERT_CHUNKS
            for c in range(2 * EXPERT_CHUNKS):
                tie = zero_row(accumulate(base - tm, sout.at[1 - par], c * per1, (c + 1) * per1), cw)
                hbuf[0:m, c * cw:(c + 1) * cw] = (
                    jnp.dot(xb, w1_ref[:, c * cw:(c + 1) * cw], preferred_element_type=F32)
                    + (b1_ref[:, c * cw:(c + 1) * cw] + tie))
            out = None
            for c in range(EXPERT_CHUNKS):
                tie = zero_row(gather(base + tm, sin.at[1 - par], c * per2, (c + 1) * per2), cw)
                glu = jnp.minimum(hbuf[0:m, c * cw:(c + 1) * cw], SWIGLU_LIMIT)
                lin = jnp.clip(hbuf[0:m, D_FF + c * cw:D_FF + (c + 1) * cw] + tie, -SWIGLU_LIMIT, SWIGLU_LIMIT)
                act = glu * jax.nn.sigmoid(SWIGLU_ALPHA * glu) * (lin + 1.0)
                part = jnp.dot(act.astype(BF16), w2_ref[c * cw:(c + 1) * cw, :], preferred_element_type=F32)
                out = part if out is None else out + part
            out = out + b2_ref[...]
            for j in range(rt):
                so[pl.ds(j * sp, m), :] = out[:, j * LANES:(j + 1) * LANES]

        quarters = (info >> 3) & 3
        for q in range(ROW_QUARTERS):
            pl.when(quarters == q)(functools.partial(expert, (q + 1) * tm // ROW_QUARTERS))

        @pl.when(last)
        def _():
            accumulate(base, so)
            crows = FINAL_CHUNK * rt

            def finish(c, carry):
                c0 = pl.multiple_of(c * crows, crows)
                r = jnp.concatenate(
                    [DEEPNORM_ALPHA * xblk[pl.ds(c0 + jj, FINAL_CHUNK, stride=rt), :]
                     + ybuf[pl.ds(c0 + jj, FINAL_CHUNK, stride=rt), :] for jj in range(rt)], axis=1)
                y = _layer_norm(r, g_ref[...], bt_ref[...])
                if tiled_out:
                    for jj in range(rt):
                        ybuf[pl.ds(c0 + jj, FINAL_CHUNK, stride=rt), :] = y[:, jj * LANES:(jj + 1) * LANES]
                    pltpu.make_async_copy(
                        ybuf.at[pl.ds(c0, crows), :],
                        o_hbm.at[pl.ds(pl.multiple_of(row0 + c * crows, crows), crows), :], osem).start()
                else:
                    p = lax.rem(c, 2)

                    @pl.when(c >= 2)
                    def _():
                        pltpu.make_async_copy(fstage.at[p], o_hbm.at[pl.ds(0, FINAL_CHUNK), :], fsem.at[p]).wait()

                    fstage[p] = y
                    tok0 = pl.multiple_of(b * tbt + c * FINAL_CHUNK, FINAL_CHUNK)
                    pltpu.make_async_copy(fstage.at[p], o_hbm.at[pl.ds(tok0, FINAL_CHUNK), :], fsem.at[p]).start()
                return carry

            lax.fori_loop(0, tbt // FINAL_CHUNK, finish, 0)
            if tiled_out:
                @pl.when(b == pl.num_programs(0) // tpb - 1)
                def _():
                    pltpu.make_async_copy(ybuf.at[pl.ds(0, blk_rows), :], o_hbm.at[pl.ds(row0, blk_rows), :],
                                          osem).wait()
            else:
                for p in range(min(2, tbt // FINAL_CHUNK)):
                    pltpu.make_async_copy(fstage.at[p], o_hbm.at[pl.ds(0, FINAL_CHUNK), :], fsem.at[p]).wait()


def _experts(h1rt, te, nact, winfo, slots_blk, gates_blk, w1b, b1, w2b, b2, ln_g, ln_b, layer, *, tm, tbt, tiled_out):
    nt = te.shape[0]
    tpb = nt // nact.shape[0]
    n_rows = h1rt.shape[0]
    out_shape = (n_rows, LANES) if tiled_out else (n_rows // ROW_TILE_ROWS, D_MODEL)
    kern = functools.partial(_expert_kernel, tm=tm, tbt=tbt, tpb=tpb, layer=layer, tiled_out=tiled_out)
    per_block = lambda i, te, na, wi: (i // tpb, 0, 0)
    per_expert = lambda i, te, na, wi: (layer, te[i], 0, 0)
    const = lambda i, te, na, wi: (0, 0)
    grid_spec = pltpu.PrefetchScalarGridSpec(
        num_scalar_prefetch=3,
        grid=(nt,),
        in_specs=[
            pl.BlockSpec((1, 1, slots_blk.shape[2]), per_block, memory_space=pltpu.SMEM,
                         pipeline_mode=pl.Buffered(1)),
            pl.BlockSpec((1, 1, gates_blk.shape[2]), per_block, memory_space=pltpu.SMEM,
                         pipeline_mode=pl.Buffered(1)),
            pl.BlockSpec(memory_space=pl.ANY),
            pl.BlockSpec(memory_space=pl.ANY),
            pl.BlockSpec((None, None, 1, 2 * D_FF), per_expert),
            pl.BlockSpec(memory_space=pl.ANY),
            pl.BlockSpec((None, None, 1, D_MODEL), per_expert),
            pl.BlockSpec((1, D_MODEL), const),
            pl.BlockSpec((1, D_MODEL), const),
        ],
        out_specs=pl.BlockSpec(memory_space=pl.ANY),
        scratch_shapes=[
            pltpu.VMEM(((tbt + 1) * ROW_TILE_ROWS, LANES), F32),
            pltpu.VMEM(((tbt + 1) * ROW_TILE_ROWS, LANES), F32),
            pltpu.VMEM((2, tm * ROW_TILE_ROWS, LANES), F32),
            pltpu.VMEM((2, ROW_TILE_ROWS * STAGE_PITCH, LANES), F32),
            pltpu.VMEM((tm, 2 * D_FF), F32),
            pltpu.VMEM((2, D_MODEL, 2 * D_FF), BF16),
            pltpu.VMEM((2, D_FF, D_MODEL), BF16),
            pltpu.VMEM((2, FINAL_CHUNK, D_MODEL), F32),
            pltpu.SemaphoreType.DMA,
            pltpu.SemaphoreType.DMA,
            pltpu.SemaphoreType.DMA((2,)),
            pltpu.SemaphoreType.DMA((2,)),
        ],
    )
    return pl.pallas_call(
        kern,
        grid_spec=grid_spec,
        out_shape=jax.ShapeDtypeStruct(out_shape, F32),
        compiler_params=pltpu.CompilerParams(
            dimension_semantics=("arbitrary",), vmem_limit_bytes=VMEM_LIMIT_BYTES),
        name="experts",
    )(te, nact, winfo, slots_blk, gates_blk, h1rt, w1b, b1, w2b, b2, ln_g, ln_b)


def _key_shift(n, tm):
    return max((n * TOP_K - 1).bit_length(), (N_EXPERTS * tm - 1).bit_length()) + 1


def _routing_plan(keys, counts, n, tm, tbt):
    nb = n // tbt
    tpb = tbt * TOP_K // tm + N_EXPERTS
    shift = _key_shift(n, tm)
    flag = 1 << (shift - 1)
    need = (-counts) % tm
    b_ids = jnp.arange(nb, dtype=jnp.int32)[:, None, None]
    e_ids = jnp.arange(N_EXPERTS, dtype=jnp.int32)[None, :, None]
    j_ids = jnp.arange(tm, dtype=jnp.int32)[None, None, :]
    pad_keys = jnp.where(j_ids < need[:, :, None],
                         (b_ids * GROUP_SLOTS + e_ids) * (1 << shift) + flag + j_ids,
                         (b_ids * GROUP_SLOTS + N_EXPERTS) * (1 << shift) + flag + e_ids * tm + j_ids)
    ks = jnp.sort(jnp.concatenate([keys.reshape(nb, tbt * TOP_K), pad_keys.reshape(nb, N_EXPERTS * tm)], axis=1),
                  axis=1)
    low = ks & (flag - 1)
    real = (ks & flag) == 0
    slot = jnp.where(real, low & (tbt * TOP_K - 1), tbt * TOP_K)
    edge = jnp.full((nb, tm), tbt * TOP_K, jnp.int32)
    slots_blk = jnp.concatenate([edge, slot, edge], axis=1).reshape(nb, 1, (tpb + 2) * tm)
    e_t = (ks[:, ::tm] >> shift) % GROUP_SLOTS
    active = e_t < N_EXPERTS
    nact = jnp.sum(active.astype(jnp.int32), axis=1)
    active = active.reshape(-1)
    te = jnp.where(active, e_t.reshape(-1), N_EXPERTS - 1)
    nt = nb * tpb
    tile = jnp.arange(nt, dtype=jnp.int32)
    prev_same = (tile % tpb != 0) & (jnp.roll(te, 1) == te) & jnp.roll(active, 1)
    gstart = active & ~prev_same
    gidx = jnp.cumsum(gstart.astype(jnp.int32)) - 1
    starts = jnp.where(gstart, tile, nt)
    nxt = jnp.flip(lax.cummin(jnp.flip(jnp.roll(starts, -1).at[-1].set(nt))))
    has_next = nxt < nt
    tnext = te[jnp.minimum(nxt, nt - 1)]
    n_real = jnp.sum(real.reshape(nt, tm).astype(jnp.int32), axis=1)
    quarters = jnp.clip((n_real + tm // ROW_QUARTERS - 1) // (tm // ROW_QUARTERS), 1, ROW_QUARTERS) - 1
    winfo = (gstart.astype(jnp.int32) | (has_next.astype(jnp.int32) << 1) | ((gidx & 1) << 2)
             | (quarters << 3) | (tnext << 5))
    return te, nact, winfo, slots_blk


def _pick_tile(total, want):
    t = min(total, want)
    while total % t:
        t //= 2
    return t


def kernel(x, mem, ln_in_g, ln_in_b, mem_ln_g, mem_ln_b, w_in, conv_w, pool_w, pool_scale, gmlp_ln_g, gmlp_ln_b,
           gmlp_w, gmlp_b, w_mem_kv, group_norm_g, w_out, ln1_g, ln1_b, w_router, b_router, w1, b1, w2, b2,
           ln2_g, ln2_b):
    bsz, seq, d = x.shape
    depth = w_in.shape[0]
    n = bsz * seq
    dg = D_GROUP
    ts = _pick_tile(seq, 512)
    tm = EXPERT_TILE
    tbt = _pick_tile(n, EXPERT_BLOCK_TOKENS)
    nb = n // tbt
    assert N_EXPERTS == 32 and tbt % ts == 0 and tbt & (tbt - 1) == 0 and tbt % FINAL_CHUNK == 0
    key_shift = _key_shift(n, tm)
    block_shift = tbt.bit_length() - 1

    h = _ln_rows(x.reshape(n, d), ln_in_g, ln_in_b, _pick_tile(n, 1024))

    wkv_all = jnp.transpose(w_mem_kv, (1, 0, 2)).reshape(d, depth * 2 * dg).astype(BF16)
    kv_all = _mem_kv(mem, mem_ln_g, mem_ln_b, wkv_all)

    pos = jnp.arange(GMLP_BLOCK)
    mask = (pos[None, :] // CHUNK) <= (pos[:, None] // CHUNK)
    eye = jnp.eye(len(POOL_WINDOWS), dtype=F32)

    wr_pad = jnp.zeros((depth, d, LANES), F32).at[:, :, :N_EXPERTS].set(w_router)
    wr_hi = wr_pad.astype(BF16)
    wr_lo = wr_hi + jnp.roll((wr_pad - wr_hi.astype(F32)).astype(BF16), N_EXPERTS, axis=2)
    br_pad = jnp.full((depth, 1, LANES), NEG_BIG, F32).at[:, 0, :N_EXPERTS].set(b_router)
    w1b = w1.astype(BF16)
    w2b = w2.astype(BF16)
    b1r = b1.reshape(depth, N_EXPERTS, 1, 2 * D_FF)
    b2r = b2.reshape(depth, N_EXPERTS, 1, d)

    for l in range(depth):
        pool_bd = jnp.einsum('gh,gcd->gchd', eye, pool_w[l]).reshape(dg, dg).astype(BF16)
        gw = jnp.where(mask[None], gmlp_w[l], 0.0).reshape(GMLP_HEADS * GMLP_BLOCK, GMLP_BLOCK).astype(BF16)
        gbias = jnp.repeat(gmlp_b[l].T, GMLP_HEAD_DIM, axis=1)
        lw = (w_in[l].astype(BF16), conv_w[l], pool_bd, pool_scale[l].reshape(1, dg),
              gmlp_ln_g[l].reshape(1, dg), gmlp_ln_b[l].reshape(1, dg), gw, gbias,
              group_norm_g[l].reshape(1, d), w_out[l].astype(BF16),
              ln1_g[l].reshape(1, d), ln1_b[l].reshape(1, d), wr_hi[l], wr_lo[l], br_pad[l])
        h1rt, keys, gates, cnt = _mixer(h, kv_all, l, lw, bsz=bsz, seq=seq, ts=ts,
                                        key_shift=key_shift, block_shift=block_shift)
        counts = jnp.sum(cnt.reshape(nb, tbt // ts, LANES), axis=1)[:, :N_EXPERTS].astype(jnp.int32)
        te, nact, winfo, slots_blk = _routing_plan(keys[:, :TOP_K].reshape(-1), counts, n, tm, tbt)
        gates_blk = jnp.pad(gates[:, :TOP_K].reshape(nb, 1, tbt * TOP_K), ((0, 0), (0, 0), (0, LANES)))
        h = _experts(h1rt, te, nact, winfo, slots_blk, gates_blk, w1b, b1r, w2b, b2r,
                     ln2_g[l].reshape(1, d), ln2_b[l].reshape(1, d), l,
                     tm=tm, tbt=tbt, tiled_out=l + 1 < depth)
    return h.reshape(bsz, seq, d)
```
